```python
import math
import jax, jax.numpy as jnp
from jax import lax
import numpy as np

D_MODEL = 4096
BATCH = 16
SEQ = 256
DEPTH = 4
DEC_BATCH = 2
DEC_SEQ = 4096
PAST_LEN = 512

GRID_W = 64
N_MIXERS = 3
N_POOL = (DEPTH + 2) // 3
N_HYENA = (DEPTH + 1) // 3
N_ATTN = DEPTH // 3
POOL_WINDOWS = (2, 4, 8, 16)
POOL_GROUPS = len(POOL_WINDOWS)
POOL_GROUP = D_MODEL // POOL_GROUPS
HYENA_ORDER = 2
HYENA_EMB = 33
HYENA_BANDS = (HYENA_EMB - 1) // 2
HYENA_FFN = 64
HYENA_SLOW = -math.log(1e-2) / 1.5
HYENA_FAST = -math.log(1e-2) / 0.3
SHORT_CONV = 3
ATTN_HEADS = 16
HEAD_DIM = D_MODEL // (2 * ATTN_HEADS)
ROPE_AXIS_DIM = HEAD_DIM // 2
ROPE_BASE = 10000.0
Q_BLOCK = 128
D_FF = 256 * ((8 * D_MODEL // 3 + 255) // 256)
FFN_CONV = 3
ALPHA = (2 * DEPTH) ** 0.25
BETA = (8 * DEPTH) ** -0.25
LN_EPS = 1e-6
SUBLN_EPS = 1e-5
F32 = jnp.float32

kernel_name = 'hybrid_pool_hyena_diffattn_dit_step'


def _layer_norm(x, g, b):
    x32 = x.astype(F32)
    mu = jnp.mean(x32, -1, keepdims=True)
    var = jnp.mean(jnp.square(x32 - mu), -1, keepdims=True)
    return ((x32 - mu) * lax.rsqrt(var + LN_EPS) * g.astype(F32) + b.astype(F32)).astype(x.dtype)


def _dwconv(x, w):
    K = w.shape[0]
    pad = K // 2
    L = x.shape[1]
    xp = jnp.pad(x, ((0, 0), (pad, K - 1 - pad), (0, 0)))
    return sum(xp[:, j:j + L] * w[j] for j in range(K))


def _pool_mixer(x, w_groups, scale):
    B, L, D = x.shape
    x32 = x.astype(F32).reshape(B, L, POOL_GROUPS, POOL_GROUP)
    cs = jnp.concatenate([jnp.zeros((B, 1, POOL_GROUPS, POOL_GROUP), F32), jnp.cumsum(x32, axis=1)], axis=1)
    t = jnp.arange(L)
    outs = []
    for g, w in enumerate(POOL_WINDOWS):
        lo = jnp.clip(t - w // 2, 0, L)
        hi = jnp.clip(t + w - w // 2, 0, L)
        cnt = (hi - lo).astype(F32)[None, :, None]
        outs.append((cs[:, hi, g] - cs[:, lo, g]) / cnt - x32[:, :, g])
    p = jnp.stack(outs, axis=2).astype(x.dtype)
    y = jnp.einsum('blgc,gcd->blgd', p, w_groups).reshape(B, L, D)
    return y * scale


def _hyena_filter(L, w1, b1, w2, b2, w3, b3, freq, decay):
    t = jnp.arange(L, dtype=F32)
    t_unit = t / max(L - 1, 1)
    bands = jnp.linspace(1e-4, HYENA_BANDS - 1, HYENA_BANDS, dtype=F32)
    ang = (2.0 * math.pi / L) * t[:, None] * bands[None, :]
    z = jnp.concatenate([t_unit[:, None], jnp.cos(ang), -jnp.sin(ang)], -1).astype(w1.dtype)
    h = jnp.sin(freq[0] * (z @ w1 + b1))
    h = jnp.sin(freq[1] * (h @ w2 + b2))
    h = (h @ w3 + b3).astype(F32).reshape(L, 2, HYENA_ORDER, D_MODEL)
    h = h * jnp.exp(-t_unit[:, None, None, None] * jnp.abs(decay.astype(F32))[None])
    fwd, bwd = h[:, 0], h[:, 1]
    k = jnp.concatenate([fwd, jnp.zeros_like(fwd[:1]), bwd[:0:-1]], axis=0)
    k = k / (jnp.sum(jnp.abs(k), axis=0, keepdims=True) + 1e-6)
    return jnp.fft.rfft(k, n=2 * L, axis=0)


def _fft_conv(u, kf):
    L = u.shape[1]
    U = jnp.fft.rfft(u.astype(F32), n=2 * L, axis=1)
    y = jnp.fft.irfft(U * kf[None], n=2 * L, axis=1)[:, :L]
    return y.astype(u.dtype)


def _hyena_mixer(x, w_in, conv_w, w1, b1, w2, b2, w3, b3, freq, decay, skip, w_out):
    L = x.shape[1]
    p = _dwconv(x @ w_in, conv_w)
    v, x1, x2 = jnp.split(p, 3, axis=-1)
    kf = _hyena_filter(L, w1, b1, w2, b2, w3, b3, freq, decay)
    z = x1 * (_fft_conv(v, kf[:, 0]) + v * skip[0])
    z = x2 * (_fft_conv(z, kf[:, 1]) + z * skip[1])
    return z @ w_out


def _rope_tables(L):
    rows = L // GRID_W
    pos_r = jnp.repeat(jnp.arange(rows, dtype=F32), GRID_W)
    pos_c = jnp.tile(jnp.arange(GRID_W, dtype=F32), rows)
    inv = jnp.power(ROPE_BASE, -jnp.arange(0, ROPE_AXIS_DIM, 2, dtype=F32) / ROPE_AXIS_DIM)
    ang_r = pos_r[:, None] * inv[None]
    ang_c = pos_c[:, None] * inv[None]
    return (jnp.cos(ang_r), jnp.sin(ang_r), jnp.cos(ang_c), jnp.sin(ang_c))


def _rot_half(x, cos, sin):
    cos = cos[None, :, None, None, :].astype(x.dtype)
    sin = sin[None, :, None, None, :].astype(x.dtype)
    a, b = jnp.split(x, 2, axis=-1)
    return jnp.concatenate([a * cos - b * sin, b * cos + a * sin], axis=-1)


def _rope_2d(x, tabs):
    cr, sr, cc, sc = tabs
    xr, xc = jnp.split(x, 2, axis=-1)
    return jnp.concatenate([_rot_half(xr, cr, sr), _rot_half(xc, cc, sc)], axis=-1)


def _diff_softmax_attend(q, k, v, lam):
    B, Lq = q.shape[:2]
    nblk = Lq // Q_BLOCK
    qb = jnp.moveaxis(q.reshape(B, nblk, Q_BLOCK, ATTN_HEADS, 2, HEAD_DIM), 1, 0)
    scale = HEAD_DIM ** -0.5

    def one_block(qblk):
        s = jnp.einsum('bqhcd,bkhcd->bhcqk', qblk, k).astype(F32) * scale
        p = jax.nn.softmax(s, axis=-1)
        a = p[:, :, 0] - lam * p[:, :, 1]
        return jnp.einsum('bhqk,bkhe->bqhe', a.astype(v.dtype), v)

    o = lax.map(one_block, qb)
    return jnp.moveaxis(o, 0, 1).reshape(B, Lq, ATTN_HEADS, 2 * HEAD_DIM)


def _diff_attention(x, w_qkv, lam_p, subln, w_out, lam_init, tabs, ctx_k, ctx_v):
    B, L, D = x.shape
    qkv = (x @ w_qkv).reshape(B, L, 3, ATTN_HEADS, 2, HEAD_DIM)
    q, k = qkv[:, :, 0], qkv[:, :, 1]
    v = qkv[:, :, 2].reshape(B, L, ATTN_HEADS, 2 * HEAD_DIM)
    k_own = k.reshape(B, L, ATTN_HEADS, 2 * HEAD_DIM)
    if ctx_k is None:
        k_all, v_all = k, v
    else:
        Lc = ctx_k.shape[1]
        q = _rope_2d(q, tabs)
        k_all = jnp.concatenate([_rope_2d(k, tabs), ctx_k.reshape(B, Lc, ATTN_HEADS, 2, HEAD_DIM)], axis=1)
        v_all = jnp.concatenate([v, ctx_v], axis=1)
    lp = lam_p.astype(F32)
    lam = jnp.exp(jnp.sum(lp[0] * lp[1])) - jnp.exp(jnp.sum(lp[2] * lp[3])) + lam_init
    o = _diff_softmax_attend(q, k_all, v_all, lam).astype(F32)
    o = o * lax.rsqrt(jnp.mean(jnp.square(o), -1, keepdims=True) + SUBLN_EPS) * subln.astype(F32) * (1.0 - lam_init)
    y = o.astype(x.dtype).reshape(B, L, D) @ w_out
    return y, k_own, v


def _conv_ffn(x, w_up, conv_w, w_down):
    h = _dwconv(x @ w_up, conv_w)
    g, u = jnp.split(h, 2, axis=-1)
    return (jax.nn.silu(g) * u) @ w_down


def _trunk(x, cond, ctx_k, ctx_v, p):
    latent = ctx_k is not None
    L = x.shape[1]
    tabs = _rope_tables(L) if latent else None
    new_k, new_v = [], []
    for i in range(DEPTH):
        kind, j = i % N_MIXERS, i // N_MIXERS
        mods = (jax.nn.silu(cond) @ p['mod_w'][i] + p['mod_b'][i])[:, None, :]
        sh1, sc1, g1, sh2, sc2, g2 = jnp.split(mods, 6, axis=-1)
        h = x * (1 + sc1) + sh1
        if kind == 0:
            y = _pool_mixer(h, p['pool_w'][j], p['pool_scale'][j])
        elif kind == 1:
            y = _hyena_mixer(h, p['hyena_w_in'][j], p['hyena_conv'][j], p['hyena_ffn_w1'][j], p['hyena_ffn_b1'][j],
                             p['hyena_ffn_w2'][j], p['hyena_ffn_b2'][j], p['hyena_ffn_w3'][j], p['hyena_ffn_b3'][j],
                             p['hyena_freq'][j], p['hyena_decay'][j], p['hyena_skip'][j], p['hyena_w_out'][j])
        else:
            lam_init = 0.8 - 0.6 * math.exp(-0.3 * i)
            y, k_own, v_own = _diff_attention(h, p['attn_w_qkv'][j], p['attn_lambda'][j], p['attn_subln'][j],
                                              p['attn_w_out'][j], lam_init, tabs,
                                              ctx_k[:, j] if latent else None, ctx_v[:, j] if latent else None)
            if not latent:
                new_k.append(k_own)
                new_v.append(v_own)
        x = _layer_norm(ALPHA * x + g1 * y, p['ln1_g'][i], p['ln1_b'][i])
        h = x * (1 + sc2) + sh2
        x = _layer_norm(ALPHA * x + g2 * _conv_ffn(h, p['ffn_w_up'][i], p['ffn_conv'][i], p['ffn_w_down'][i]),
                        p['ln2_g'][i], p['ln2_b'][i])
    return x, new_k, new_v


def setup_inputs(seed: int = 0) -> dict:
    key = jax.random.key(seed)
    ks = iter(jax.random.split(key, 48))

    def nrm(shape, s):
        return jax.random.normal(next(ks), shape, F32) * s

    D, F, H, dh = D_MODEL, D_FF, ATTN_HEADS, HEAD_DIM
    decay_base = jnp.linspace(HYENA_SLOW, HYENA_FAST, D, dtype=F32)
    return {
        'x_prompt': nrm((BATCH, SEQ, D), 1.0),
        'x_sample': nrm((DEC_BATCH, DEC_SEQ, D), 1.0),
        'cache_k': nrm((DEC_BATCH, N_ATTN, PAST_LEN, H, 2 * dh), 1.0),
        'cache_v': nrm((DEC_BATCH, N_ATTN, PAST_LEN, H, 2 * dh), 1.0),
        'c': nrm((DEC_BATCH, D), 1.0),
        'c_ctx': nrm((D,), 1.0),
        'mod_w': nrm((DEPTH, D, 6 * D), 0.5 * D ** -0.5),
        'mod_b': nrm((DEPTH, 6 * D), 0.01),
        'ln1_g': 1.0 + nrm((DEPTH, D), 0.02),
        'ln1_b': nrm((DEPTH, D), 0.02),
        'ln2_g': 1.0 + nrm((DEPTH, D), 0.02),
        'ln2_b': nrm((DEPTH, D), 0.02),
        'pool_w': nrm((N_POOL, POOL_GROUPS, POOL_GROUP, POOL_GROUP), BETA * POOL_GROUP ** -0.5),
        'pool_scale': 1.0 + nrm((N_POOL, D), 0.1),
        'hyena_w_in': nrm((N_HYENA, D, 3 * D), D ** -0.5),
        'hyena_conv': nrm((N_HYENA, SHORT_CONV, 3 * D), SHORT_CONV ** -0.5),
        'hyena_ffn_w1': nrm((N_HYENA, HYENA_EMB, HYENA_FFN), HYENA_EMB ** -0.5),
        'hyena_ffn_b1': nrm((N_HYENA, HYENA_FFN), 0.1),
        'hyena_ffn_w2': nrm((N_HYENA, HYENA_FFN, HYENA_FFN), HYENA_FFN ** -0.5),
        'hyena_ffn_b2': nrm((N_HYENA, HYENA_FFN), 0.1),
        'hyena_ffn_w3': nrm((N_HYENA, HYENA_FFN, 2 * HYENA_ORDER * D), HYENA_FFN ** -0.5),
        'hyena_ffn_b3': nrm((N_HYENA, 2 * HYENA_ORDER * D), 0.1),
        'hyena_freq': 1.0 + nrm((N_HYENA, 2, HYENA_FFN), 0.1),
        'hyena_decay': decay_base * (1.0 + nrm((N_HYENA, 2, HYENA_ORDER, D), 0.05)),
        'hyena_skip': nrm((N_HYENA, HYENA_ORDER, D), 1.0),
        'hyena_w_out': nrm((N_HYENA, D, D), BETA * D ** -0.5),
        'attn_w_qkv': nrm((N_ATTN, D, 3 * D), D ** -0.5),
        'attn_lambda': nrm((N_ATTN, 4, dh), 0.1),
        'attn_subln': 1.0 + nrm((N_ATTN, 2 * dh), 0.02),
        'attn_w_out': nrm((N_ATTN, D, D), BETA * D ** -0.5),
        'ffn_w_up': nrm((DEPTH, D, 2 * F), D ** -0.5),
        'ffn_conv': nrm((DEPTH, FFN_CONV, 2 * F), FFN_CONV ** -0.5),
        'ffn_w_down': nrm((DEPTH, F, D), BETA * F ** -0.5),
    }


def reference(x_prompt, x_sample, cache_k, cache_v, c, c_ctx, mod_w, mod_b, ln1_g, ln1_b, ln2_g, ln2_b,
              pool_w, pool_scale, hyena_w_in, hyena_conv, hyena_ffn_w1, hyena_ffn_b1, hyena_ffn_w2, hyena_ffn_b2,
              hyena_ffn_w3, hyena_ffn_b3, hyena_freq, hyena_decay, hyena_skip, hyena_w_out,
              attn_w_qkv, attn_lambda, attn_subln, attn_w_out, ffn_w_up, ffn_conv, ffn_w_down):
    p = dict(mod_w=mod_w, mod_b=mod_b, ln1_g=ln1_g, ln1_b=ln1_b, ln2_g=ln2_g, ln2_b=ln2_b,
             pool_w=pool_w, pool_scale=pool_scale, hyena_w_in=hyena_w_in, hyena_conv=hyena_conv,
             hyena_ffn_w1=hyena_ffn_w1, hyena_ffn_b1=hyena_ffn_b1, hyena_ffn_w2=hyena_ffn_w2,
             hyena_ffn_b2=hyena_ffn_b2, hyena_ffn_w3=hyena_ffn_w3, hyena_ffn_b3=hyena_ffn_b3,
             hyena_freq=hyena_freq, hyena_decay=hyena_decay, hyena_skip=hyena_skip, hyena_w_out=hyena_w_out,
             attn_w_qkv=attn_w_qkv, attn_lambda=attn_lambda, attn_subln=attn_subln, attn_w_out=attn_w_out,
             ffn_w_up=ffn_w_up, ffn_conv=ffn_conv, ffn_w_down=ffn_w_down)
    y_prompt, ks, vs = _trunk(x_prompt, c_ctx[None, :], None, None, p)
    new_cache_k = jnp.stack(ks, axis=1)
    new_cache_v = jnp.stack(vs, axis=1)
    y_sample, _, _ = _trunk(x_sample, c, cache_k, cache_v, p)
    return (y_prompt, y_sample, new_cache_k, new_cache_v)
```

```python
import functools
import math

import jax
import jax.numpy as jnp
from jax import lax
from jax.experimental import pallas as pl
from jax.experimental.pallas import tpu as pltpu

F32 = jnp.float32
BF16 = jnp.bfloat16

GRID_COLS = 64
POOL_WINDOW_SIZES = (2, 4, 8, 16)
ROPE_THETA = 10000.0
LN_EPSILON = 1e-6
SUBLN_EPSILON = 1e-5
N_MIXER_KINDS = 3

V7X_VMEM_BYTES = 64 * 1024 * 1024
V7X_VMEM_BUDGET = V7X_VMEM_BYTES - 6 * 1024 * 1024
SUBLANES_F32 = 8
SUBLANES_BF16 = 16
LANES = 128
LN_CHUNK_ROWS = 64


def _params(sem, est_bytes):
    limit = int(min(V7X_VMEM_BUDGET, max(32 * 1024 * 1024, est_bytes * 5 // 4)))
    return pltpu.CompilerParams(dimension_semantics=sem, vmem_limit_bytes=limit)


def _divisor(n, pref, mult):
    best = None
    d = mult
    while d <= min(n, pref):
        if n % d == 0:
            best = d
        d += mult
    assert best is not None, (n, pref, mult)
    return best


def _layer_norm(r, g, b):
    mu = jnp.mean(r, -1, keepdims=True)
    d = r - mu
    var = jnp.mean(d * d, -1, keepdims=True)
    return d * lax.rsqrt(var + LN_EPSILON) * g + b


def _mod_row(t0, tp, ds):
    return jnp.where(t0 < tp, 0, 1 + (t0 - tp) // ds)


def _mods_kernel(c_ref, w_ref, b_ref, o_ref):
    k = pl.program_id(2)

    @pl.when(k == 0)
    def _():
        o_ref[0] = jnp.broadcast_to(b_ref[0], o_ref.shape[1:])

    c = c_ref[...]
    s = (c * jax.nn.sigmoid(c)).astype(BF16)
    o_ref[0] += jnp.dot(s, w_ref[0].astype(BF16), preferred_element_type=F32)


def _mods_call(cond8, mod_w, mod_b):
    depth, d, n = mod_w.shape
    tk = _divisor(d, 2048, LANES)
    tn = _divisor(n, 1024, LANES)
    est = 2 * tk * tn * 4 + 4 * 8 * tn * 4 + 2 * 8 * tk * 4 + tk * tn * 2
    return pl.pallas_call(
        _mods_kernel,
        grid=(depth, n // tn, d // tk),
        in_specs=[pl.BlockSpec((8, tk), lambda l, j, k: (0, k)),
                  pl.BlockSpec((1, tk, tn), lambda l, j, k: (l, k, j)),
                  pl.BlockSpec((1, 1, tn), lambda l, j, k: (l, 0, j))],
        out_specs=pl.BlockSpec((1, 8, tn), lambda l, j, k: (l, 0, j)),
        out_shape=jax.ShapeDtypeStruct((depth, 8, n), F32),
        compiler_params=_params(("parallel", "parallel", "arbitrary"), est),
        name="mods",
    )(cond8, mod_w, mod_b.reshape(depth, 1, n))


def _pool_kernel(xm_ref, xp_ref, xn_ref, mod_ref, pw_ref, ps_ref, lg_ref, lb_ref, xo_ref, ho_ref, e_ref,
                 *, tm, tp, s_len, ds, d, alpha):
    t0 = pl.program_id(0) * tm
    is_ctx = t0 < tp
    seq_len = jnp.where(is_ctx, s_len, ds)
    pos0 = jnp.where(is_ctx, t0 % s_len, (t0 - tp) % ds)
    mods = mod_ref[...]
    sh1, sc1, g1 = mods[:, 0:d], mods[:, d:2 * d], mods[:, 2 * d:3 * d]
    sh2, sc2 = mods[:, 3 * d:4 * d], mods[:, 4 * d:5 * d]
    x = xm_ref[...]
    h = x * (1.0 + sc1) + sh1
    has_prev = pos0 > 0
    has_next = pos0 + tm < seq_len
    e_ref[0:8, :] = jnp.where(has_prev, xp_ref[...] * (1.0 + sc1) + sh1, 0.0)
    e_ref[8:8 + tm, :] = h
    e_ref[8 + tm:16 + tm, :] = jnp.where(has_next, xn_ref[...] * (1.0 + sc1) + sh1, 0.0)
    t = pos0 + lax.broadcasted_iota(jnp.int32, (tm, 1), 0)
    cg = d // len(POOL_WINDOW_SIZES)
    ys = []
    for g, w in enumerate(POOL_WINDOW_SIZES):
        c0 = g * cg
        acc = None
        for s in range(-(w // 2), w - w // 2):
            v = e_ref[8 + s:8 + s + tm, c0:c0 + cg]
            acc = v if acc is None else acc + v
        lo = jnp.maximum(t - w // 2, 0)
        hi = jnp.minimum(t + (w - w // 2), seq_len)
        inv = 1.0 / (hi - lo).astype(F32)
        p = acc * inv - h[:, c0:c0 + cg]
        ys.append(jnp.dot(p.astype(BF16), pw_ref[g], preferred_element_type=F32))
    y = jnp.concatenate(ys, axis=1) * ps_ref[...]
    xn = _layer_norm(alpha * x + g1 * y, lg_ref[...], lb_ref[...])
    xo_ref[...] = xn
    ho_ref[...] = (xn * (1.0 + sc2) + sh2).astype(BF16)


def _pool_call(x, mods4, layer, pw, ps, lg, lb, dims, alpha):
    t, d = x.shape
    tp, s_len, ds = dims
    tm = s_len
    assert ds % tm == 0 and tm % SUBLANES_BF16 == 0
    n8 = t // 8
    kern = functools.partial(_pool_kernel, tm=tm, tp=tp, s_len=s_len, ds=ds, d=d, alpha=alpha)
    g, cg, _ = pw.shape
    est = (2 * tm * d * 4 * 2 + 2 * tm * d * 2 + (tm + 16) * d * 4 + 2 * g * cg * cg * 2
           + 6 * tm * d * 4)
    return pl.pallas_call(
        kern,
        grid=(t // tm,),
        in_specs=[pl.BlockSpec((tm, d), lambda i: (i, 0)),
                  pl.BlockSpec((8, d), lambda i: (jnp.maximum(i * (tm // 8) - 1, 0), 0)),
                  pl.BlockSpec((8, d), lambda i: (jnp.minimum((i + 1) * (tm // 8), n8 - 1), 0)),
                  pl.BlockSpec((None, None, 1, 6 * d), lambda i: (layer, _mod_row(i * tm, tp, ds), 0, 0)),
                  pl.BlockSpec((g, cg, cg), lambda i: (0, 0, 0)),
                  pl.BlockSpec((1, d), lambda i: (0, 0)),
                  pl.BlockSpec((1, d), lambda i: (0, 0)),
                  pl.BlockSpec((1, d), lambda i: (0, 0))],
        out_specs=[pl.BlockSpec((tm, d), lambda i: (i, 0)),
                   pl.BlockSpec((tm, d), lambda i: (i, 0))],
        out_shape=[jax.ShapeDtypeStruct((t, d), F32), jax.ShapeDtypeStruct((t, d), BF16)],
        scratch_shapes=[pltpu.VMEM((tm + 16, d), F32)],
        compiler_params=_params(("parallel",), est),
        name="pool_mixer",
    )(x, x, x, mods4, pw, ps.reshape(1, d), lg.reshape(1, d), lb.reshape(1, d))


HALO = SUBLANES_BF16


def _fill_ext(ext_ref, a_ref, ap_ref, an_ref, tm):
    @pl.when(pl.program_id(1) == 0)
    def _():
        ext_ref[0:HALO, :] = ap_ref[...]
        ext_ref[HALO:HALO + tm, :] = a_ref[...]
        ext_ref[HALO + tm:2 * HALO + tm, :] = an_ref[...]


def _conv3(p, cw, not_first, not_last, tm):
    rows = p.shape[0]
    up = pltpu.roll(p, 1, 0)[HALO:HALO + tm]
    dn = pltpu.roll(p, rows - 1, 0)[HALO:HALO + tm]
    mid = p[HALO:HALO + tm]
    up = jnp.where(not_first > 0.0, up, 0.0)
    dn = jnp.where(not_last > 0.0, dn, 0.0)
    return cw[0:1] * up + cw[1:2] * mid + cw[2:3] * dn


def _up_ffn_kernel(a_ref, ap_ref, an_ref, edge_ref, wg_ref, wu_ref, cg_ref, cu_ref, o_ref, ext_ref, *, tm):
    _fill_ext(ext_ref, a_ref, ap_ref, an_ref, tm)
    e = edge_ref[...]
    nf, nl = e[:, 0:1], e[:, 1:2]
    a = ext_ref[...]
    g = _conv3(jnp.dot(a, wg_ref[...], preferred_element_type=F32), cg_ref[...], nf, nl, tm)
    u = _conv3(jnp.dot(a, wu_ref[...], preferred_element_type=F32), cu_ref[...], nf, nl, tm)
    o_ref[...] = (g * jax.nn.sigmoid(g) * u).astype(o_ref.dtype)


def _up_conv_kernel(a_ref, ap_ref, an_ref, edge_ref, w_ref, cw_ref, o_ref, ext_ref, *, tm):
    _fill_ext(ext_ref, a_ref, ap_ref, an_ref, tm)
    e = edge_ref[...]
    p = jnp.dot(ext_ref[...], w_ref[...], preferred_element_type=F32)
    o_ref[...] = _conv3(p, cw_ref[...], e[:, 0:1], e[:, 1:2], tm).astype(o_ref.dtype)


def _up_plain_kernel(a_ref, w_ref, o_ref):
    o_ref[...] = jnp.dot(a_ref[...], w_ref[...], preferred_element_type=F32).astype(o_ref.dtype)


def _halo_specs(tm, d, t):
    nh = t // HALO
    return [pl.BlockSpec((tm, d), lambda i, j: (i, 0)),
            pl.BlockSpec((HALO, d), lambda i, j: (jnp.maximum(i * (tm // HALO) - 1, 0), 0)),
            pl.BlockSpec((HALO, d), lambda i, j: (jnp.minimum((i + 1) * (tm // HALO), nh - 1), 0)),
            pl.BlockSpec((tm, 2), lambda i, j: (i, 0))]


def _up_ffn_call(h, edge, wg, wu, cg, cu):
    t, d = h.shape
    fp = wg.shape[1]
    tm = _divisor(t, 1024, HALO)
    tn = _divisor(fp, 512, LANES)
    est = (2 * tm * d * 2 + (tm + 2 * HALO) * d * 2 + 4 * d * tn * 2 + 2 * tm * tn * 2
           + 8 * (tm + 2 * HALO) * tn * 4)
    return pl.pallas_call(
        functools.partial(_up_ffn_kernel, tm=tm),
        grid=(t // tm, fp // tn),
        in_specs=_halo_specs(tm, d, t) + [
            pl.BlockSpec((d, tn), lambda i, j: (0, j)),
            pl.BlockSpec((d, tn), lambda i, j: (0, j)),
            pl.BlockSpec((3, tn), lambda i, j: (0, j)),
            pl.BlockSpec((3, tn), lambda i, j: (0, j))],
        out_specs=pl.BlockSpec((tm, tn), lambda i, j: (i, j)),
        out_shape=jax.ShapeDtypeStruct((t, fp), BF16),
        scratch_shapes=[pltpu.VMEM((tm + 2 * HALO, d), BF16)],
        compiler_params=_params(("parallel", "arbitrary"), est),
        name="ffn_up",
    )(h, h, h, edge, wg, wu, cg, cu)


def _up_conv_call(h, edge, w, cw):
    t, d = h.shape
    n = w.shape[1]
    tm = _divisor(t, 1024, HALO)
    tn = _divisor(n, 512, LANES)
    est = (2 * tm * d * 2 + (tm + 2 * HALO) * d * 2 + 2 * d * tn * 2 + 2 * tm * tn * 4
           + 5 * (tm + 2 * HALO) * tn * 4)
    return pl.pallas_call(
        functools.partial(_up_conv_kernel, tm=tm),
        grid=(t // tm, n // tn),
        in_specs=_halo_specs(tm, d, t) + [
            pl.BlockSpec((d, tn), lambda i, j: (0, j)),
            pl.BlockSpec((3, tn), lambda i, j: (0, j))],
        out_specs=pl.BlockSpec((tm, tn), lambda i, j: (i, j)),
        out_shape=jax.ShapeDtypeStruct((t, n), F32),
        scratch_shapes=[pltpu.VMEM((tm + 2 * HALO, d), BF16)],
        compiler_params=_params(("parallel", "arbitrary"), est),
        name="hyena_in",
    )(h, h, h, edge, w, cw)


def _up_plain_call(h, w):
    t, d = h.shape
    n = w.shape[1]
    tm = _divisor(t, 1024, HALO)
    tn = _divisor(n, 512, LANES)
    est = 2 * tm * d * 2 + 2 * d * tn * 2 + 2 * tm * tn * 4 + 2 * tm * tn * 4
    return pl.pallas_call(
        _up_plain_kernel,
        grid=(t // tm, n // tn),
        in_specs=[pl.BlockSpec((tm, d), lambda i, j: (i, 0)),
                  pl.BlockSpec((d, tn), lambda i, j: (0, j))],
        out_specs=pl.BlockSpec((tm, tn), lambda i, j: (i, j)),
        out_shape=jax.ShapeDtypeStruct((t, n), F32),
        compiler_params=_params(("parallel", "arbitrary"), est),
        name="attn_qkv",
    )(h, w)


def _down_ln_kernel(*refs, nk, d, gate, alpha, nxt):
    if nxt is None:
        a_ref, w_ref, x_ref, mod_ref, lg_ref, lb_ref, xo_ref = refs
    else:
        a_ref, w_ref, x_ref, mod_ref, modn_ref, lg_ref, lb_ref, xo_ref, ho_ref = refs
    k = pl.program_id(1)
    part = jnp.dot(a_ref[...], w_ref[...], preferred_element_type=F32)

    @pl.when(k == 0)
    def _():
        xo_ref[...] = part

    @pl.when(k > 0)
    def _():
        xo_ref[...] += part

    @pl.when(k == nk - 1)
    def _():
        g = mod_ref[:, gate * d:(gate + 1) * d]
        tm = xo_ref.shape[0]
        ch = math.gcd(tm, LN_CHUNK_ROWS)

        def chunk(r, carry):
            rows = pl.ds(pl.multiple_of(r * ch, ch), ch)
            xn = _layer_norm(alpha * x_ref[rows, :] + g * xo_ref[rows, :], lg_ref[...], lb_ref[...])
            xo_ref[rows, :] = xn
            if nxt is not None:
                sh = modn_ref[:, nxt * d:(nxt + 1) * d]
                sc = modn_ref[:, (nxt + 1) * d:(nxt + 2) * d]
                ho_ref[rows, :] = (xn * (1.0 + sc) + sh).astype(BF16)
            return carry

        lax.fori_loop(0, tm // ch, chunk, 0)


def _down_ln_call(a, w, x, mods4, layer, gate, nxt_layer, nxt, lg, lb, dims, alpha, name):
    t, kdim = a.shape
    d = w.shape[1]
    tp, s_len, ds = dims
    tm = _divisor(math.gcd(tp, ds), 512, HALO)
    tk = _divisor(kdim, 512, LANES)
    nk = kdim // tk
    kern = functools.partial(_down_ln_kernel, nk=nk, d=d, gate=gate, alpha=alpha, nxt=nxt)
    mod_spec = lambda l: pl.BlockSpec((None, None, 1, 6 * d),
                                      lambda i, k: (l, _mod_row(i * tm, tp, ds), 0, 0))
    in_specs = [pl.BlockSpec((tm, tk), lambda i, k: (i, k)),
                pl.BlockSpec((tk, d), lambda i, k: (k, 0)),
                pl.BlockSpec((tm, d), lambda i, k: (i, 0), pipeline_mode=pl.Buffered(1)),
                mod_spec(layer)]
    args = [a, w, x, mods4]
    if nxt is not None:
        in_specs.append(mod_spec(nxt_layer))
        args.append(mods4)
    in_specs += [pl.BlockSpec((1, d), lambda i, k: (0, 0)), pl.BlockSpec((1, d), lambda i, k: (0, 0))]
    args += [lg.reshape(1, d), lb.reshape(1, d)]
    out_specs = [pl.BlockSpec((tm, d), lambda i, k: (i, 0))]
    out_shape = [jax.ShapeDtypeStruct((t, d), F32)]
    if nxt is not None:
        out_specs.append(pl.BlockSpec((tm, d), lambda i, k: (i, 0)))
        out_shape.append(jax.ShapeDtypeStruct((t, d), BF16))
    est = 2 * tm * tk * 2 + 2 * tk * d * 2 + 4 * tm * d * 4 + 2 * tm * d * 2 + 3 * tm * d * 4
    res = pl.pallas_call(
        kern,
        grid=(t // tm, nk),
        in_specs=in_specs,
        out_specs=out_specs,
        out_shape=out_shape,
        compiler_params=_params(("parallel", "arbitrary"), est),
        name=name,
    )(*args)
    return (res[0], res[1]) if nxt is not None else (res[0], None)


def _dot_f32(a, b):
    return lax.dot_general(a, b, (((1,), (0,)), ((), ())), precision=lax.Precision.HIGHEST,
                           preferred_element_type=F32)


def _filter_kernel(z_ref, tu_ref, w1_ref, b1_ref, w2_ref, b2_ref, w3_ref, b3_ref, fr_ref, dec_ref,
                   hk_ref, as_ref, *, tl, n_col_tiles):
    jc = pl.program_id(0)
    i = pl.program_id(1)
    h = jnp.sin(fr_ref[0:1, :] * (_dot_f32(z_ref[...], w1_ref[...]) + b1_ref[...]))
    h = jnp.sin(fr_ref[1:2, :] * (_dot_f32(h, w2_ref[...]) + b2_ref[...]))
    h = _dot_f32(h, w3_ref[...]) + b3_ref[...]
    h = h * jnp.exp(-tu_ref[...] * jnp.abs(dec_ref[...]))
    row = i * tl + lax.broadcasted_iota(jnp.int32, (tl, 1), 0)
    drop = jnp.logical_and(row == 0, jc >= n_col_tiles // 2)
    h = jnp.where(drop, 0.0, h)
    hk_ref[...] = h.astype(BF16)
    part = jnp.sum(jnp.abs(h), axis=0, keepdims=True)

    @pl.when(i == 0)
    def _():
        as_ref[...] = part

    @pl.when(i > 0)
    def _():
        as_ref[...] += part


def _filter_call(length, w1, b1, w2, b2, w3, b3, freq, decay):
    emb, width = w1.shape
    n = w3.shape[1]
    t = jnp.arange(length, dtype=F32)
    t_unit = t / max(length - 1, 1)
    n_bands = (emb - 1) // 2
    bands = jnp.linspace(1e-4, n_bands - 1, n_bands, dtype=F32)
    ang = (2.0 * math.pi / length) * t[:, None] * bands[None, :]
    z = jnp.concatenate([t_unit[:, None], jnp.cos(ang), -jnp.sin(ang)], -1)
    z = jnp.pad(z, ((0, 0), (0, LANES - emb)))
    w1p = jnp.pad(w1, ((0, LANES - emb), (0, 0)))
    tl = _divisor(length, 256, 8)
    tn = _divisor(n // 2, 2048, LANES)
    nct = n // tn
    est = 2 * tl * tn * 2 + 6 * tl * tn * 4 + 4 * width * tn * 4
    return pl.pallas_call(
        functools.partial(_filter_kernel, tl=tl, n_col_tiles=nct),
        grid=(nct, length // tl),
        in_specs=[pl.BlockSpec((tl, LANES), lambda j, i: (i, 0)),
                  pl.BlockSpec((tl, 1), lambda j, i: (i, 0)),
                  pl.BlockSpec((LANES, width), lambda j, i: (0, 0)),
                  pl.BlockSpec((1, width), lambda j, i: (0, 0)),
                  pl.BlockSpec((width, width), lambda j, i: (0, 0)),
                  pl.BlockSpec((1, width), lambda j, i: (0, 0)),
                  pl.BlockSpec((width, tn), lambda j, i: (0, j)),
                  pl.BlockSpec((1, tn), lambda j, i: (0, j)),
                  pl.BlockSpec((2, width), lambda j, i: (0, 0)),
                  pl.BlockSpec((1, tn), lambda j, i: (0, j))],
        out_specs=[pl.BlockSpec((tl, tn), lambda j, i: (i, j)),
                   pl.BlockSpec((1, tn), lambda j, i: (0, j))],
        out_shape=[jax.ShapeDtypeStruct((length, n), BF16), jax.ShapeDtypeStruct((1, n), F32)],
        compiler_params=_params(("parallel", "arbitrary"), est),
        name="hyena_filter",
    )(z, t_unit[:, None], w1p, b1.reshape(1, width), w2, b2.reshape(1, width), w3, b3.reshape(1, n),
      freq, decay.reshape(1, n))


def _dft_tables(length, tq):
    n2 = 2 * length
    r = lax.broadcasted_iota(jnp.int32, (2 * length, length), 0)
    n = lax.broadcasted_iota(jnp.int32, (2 * length, length), 1)
    f = (r // (2 * tq)) * tq + r % tq
    is_sin = (r // tq) % 2 == 1
    m = (f * n + jnp.where(is_sin, n2 // 4, 0)) % n2
    val = jnp.cos(m.astype(F32) * (2.0 * math.pi / n2))
    nyq = jnp.where(n % 2 == 0, 1.0, -1.0)
    wf = jnp.where(jnp.logical_and(is_sin, f == 0), nyq, val).astype(BF16)
    return wf, wf.T


def _kf_kernel(wf_ref, hf_ref, hb_ref, af_ref, ab_ref, kre_ref, kim_ref, *, tq, n2):
    q = pl.program_id(1)
    ff = jnp.dot(wf_ref[...], hf_ref[...], preferred_element_type=F32)
    fb = jnp.dot(wf_ref[...], hb_ref[...], preferred_element_type=F32)
    inv = 1.0 / (af_ref[...] + ab_ref[...] + 1e-6)
    f0 = (q * tq + lax.broadcasted_iota(jnp.int32, (tq, 1), 0)) == 0
    w = jnp.where(f0, 1.0 / n2, 2.0 / n2) * inv
    kre_ref[...] = (ff[:tq] + fb[:tq]) * w
    kim_ref[...] = jnp.where(f0, ff[tq:] + fb[tq:], ff[tq:] - fb[tq:]) * w


def _kf_call(wf, hk, asum, tq):
    length = hk.shape[0]
    n = hk.shape[1] // 2
    dt = _divisor(n, 512, LANES)
    nq = length // tq
    nb = n // dt
    est = 2 * 2 * tq * length * 2 + 4 * length * dt * 2 + 4 * tq * dt * 4 + 6 * 2 * tq * dt * 4
    return pl.pallas_call(
        functools.partial(_kf_kernel, tq=tq, n2=2 * length),
        grid=(nb, nq),
        in_specs=[pl.BlockSpec((2 * tq, length), lambda j, q: (q, 0)),
                  pl.BlockSpec((length, dt), lambda j, q: (0, j)),
                  pl.BlockSpec((length, dt), lambda j, q: (0, nb + j)),
                  pl.BlockSpec((1, dt), lambda j, q: (0, j)),
                  pl.BlockSpec((1, dt), lambda j, q: (0, nb + j))],
        out_specs=[pl.BlockSpec((tq, dt), lambda j, q: (q, j)),
                   pl.BlockSpec((tq, dt), lambda j, q: (q, j))],
        out_shape=[jax.ShapeDtypeStruct((length, n), F32), jax.ShapeDtypeStruct((length, n), F32)],
        compiler_params=_params(("parallel", "arbitrary"), est),
        name="hyena_filter_dft",
    )(wf, hk, hk, asum, asum)


def _dftconv_kernel(wf_ref, wft_ref, kre_ref, kim_ref, skip_ref, u_ref, g_ref, prev_ref, o_ref, ub_ref, acc_ref,
                    *, nq, tq):
    del prev_ref
    q = pl.program_id(2)

    @pl.when(q == 0)
    def _():
        ub_ref[...] = u_ref[...].astype(BF16)
        acc_ref[...] = jnp.zeros_like(acc_ref)

    uf = jnp.dot(wf_ref[...], ub_ref[...], preferred_element_type=F32)
    ure, uim = uf[:tq], uf[tq:]
    kre, kim = kre_ref[...], kim_ref[...]
    f0 = (q * tq + lax.broadcasted_iota(jnp.int32, (tq, 1), 0)) == 0
    yre = ure * kre - jnp.where(f0, 0.0, uim * kim)
    yim = jnp.where(f0, uim * kim, ure * kim + uim * kre)
    y = jnp.concatenate([yre, yim], axis=0).astype(BF16)
    acc_ref[...] += jnp.dot(wft_ref[...], y, preferred_element_type=F32)

    @pl.when(q == nq - 1)
    def _():
        o_ref[...] = (g_ref[...] * (acc_ref[...] + u_ref[...] * skip_ref[...])).astype(o_ref.dtype)


def _dftconv_call(tabs, kre, kim, korder, skip, u, ucol, gsrc, gcol, row0, n_seq, out, out_dtype, d):
    wf, wft, tq = tabs
    length = wft.shape[0]
    nq = length // tq
    dt = _divisor(d, 256 if length > 1024 else 512, LANES)
    nd = d // dt
    est = (2 * 2 * 2 * tq * length * 2 + 4 * length * dt * 4 + 2 * length * dt * 4 + length * dt * 6
           + 8 * tq * dt * 4 + 2 * length * dt * 4)
    kern = functools.partial(_dftconv_kernel, nq=nq, tq=tq)
    return pl.pallas_call(
        kern,
        grid=(n_seq, nd, nq),
        in_specs=[pl.BlockSpec((2 * tq, length), lambda b, j, q: (q, 0)),
                  pl.BlockSpec((length, 2 * tq), lambda b, j, q: (0, q)),
                  pl.BlockSpec((tq, dt), lambda b, j, q: (q, korder * nd + j)),
                  pl.BlockSpec((tq, dt), lambda b, j, q: (q, korder * nd + j)),
                  pl.BlockSpec((1, dt), lambda b, j, q: (0, j)),
                  pl.BlockSpec((length, dt), lambda b, j, q: (row0 + b, ucol * nd + j)),
                  pl.BlockSpec((length, dt), lambda b, j, q: (row0 + b, gcol * nd + j)),
                  pl.BlockSpec(memory_space=pl.ANY)],
        out_specs=pl.BlockSpec((length, dt), lambda b, j, q: (row0 + b, j)),
        out_shape=jax.ShapeDtypeStruct(out.shape, out_dtype),
        scratch_shapes=[pltpu.VMEM((length, dt), BF16), pltpu.VMEM((length, dt), F32)],
        input_output_aliases={7: 0},
        compiler_params=_params(("parallel", "parallel", "arbitrary"), est),
        name="hyena_conv",
    )(wf, wft, kre, kim, skip, u, gsrc, out)


def _rope(x, cos, sin, dh):
    outs = []
    for c in range(2):
        xc = x[:, c * dh:(c + 1) * dh]
        lane = lax.broadcasted_iota(jnp.int32, xc.shape, 1)
        first_half = (lane & 32) == 0
        sw = jnp.where(first_half, pltpu.roll(xc, dh - 32, 1), pltpu.roll(xc, 32, 1))
        outs.append(xc * cos + sw * sin)
    return jnp.concatenate(outs, axis=1)


def _rope_kv_kernel(k_ref, v_ref, cos_ref, sin_ref, ko_ref, vo_ref, *, dh):
    ko_ref[...] = _rope(k_ref[...], cos_ref[...], sin_ref[...], dh).astype(BF16)
    vo_ref[...] = v_ref[...].astype(BF16)


def _rope_kv_call(qkv, cos, sin, tp, d, dh):
    t = qkv.shape[0]
    ts = t - tp
    length = cos.shape[0]
    tr = _divisor(math.gcd(length, tp), 512, HALO)
    nh = d // (2 * dh)
    per_seq = length // tr
    est = 8 * tr * 2 * dh * 4 + 4 * tr * dh * 4
    return pl.pallas_call(
        functools.partial(_rope_kv_kernel, dh=dh),
        grid=(ts // tr, nh),
        in_specs=[pl.BlockSpec((tr, 2 * dh), lambda i, h: (tp // tr + i, nh + h)),
                  pl.BlockSpec((tr, 2 * dh), lambda i, h: (tp // tr + i, 2 * nh + h)),
                  pl.BlockSpec((tr, dh), lambda i, h: (i % per_seq, 0)),
                  pl.BlockSpec((tr, dh), lambda i, h: (i % per_seq, 0))],
        out_specs=[pl.BlockSpec((tr, 2 * dh), lambda i, h: (i, h)),
                   pl.BlockSpec((tr, 2 * dh), lambda i, h: (i, h))],
        out_shape=[jax.ShapeDtypeStruct((ts, d), BF16), jax.ShapeDtypeStruct((ts, d), BF16)],
        compiler_params=_params(("parallel", "parallel"), est),
        name="attn_rope_kv",
    )(qkv, qkv, cos, sin)


def _attn_kernel(*refs, latent, dh, lam_init):
    if latent:
        lam_ref, sub_ref, q_ref, k_ref, v_ref, ck_ref, cv_ref, cos_ref, sin_ref, o_ref = refs
    else:
        lam_ref, sub_ref, q_ref, k_ref, v_ref, o_ref = refs
    scale = dh ** -0.5
    lp = lam_ref[...]
    lam = (jnp.exp(jnp.sum(lp[0:1] * lp[1:2], axis=1, keepdims=True))
           - jnp.exp(jnp.sum(lp[2:3] * lp[3:4], axis=1, keepdims=True)) + lam_init)
    q = q_ref[...]
    if latent:
        q = _rope(q, cos_ref[...], sin_ref[...], dh)
    qb = q.astype(BF16)
    ks = [k_ref[...].astype(BF16)]
    vs = [v_ref[...].astype(BF16)]
    if latent:
        ks.append(ck_ref[...].astype(BF16))
        vs.append(cv_ref[...].astype(BF16))
    nt = (((1,), (1,)), ((), ()))
    weights = None
    for c in range(2):
        qc = qb[:, c * dh:(c + 1) * dh]
        ss = [lax.dot_general(qc, kk[:, c * dh:(c + 1) * dh], nt, preferred_element_type=F32) for kk in ks]
        m = functools.reduce(jnp.maximum, [jnp.max(s, axis=1, keepdims=True) for s in ss])
        es = [jnp.exp((s - m) * scale) for s in ss]
        l = functools.reduce(lambda a, b: a + b, [jnp.sum(e, axis=1, keepdims=True) for e in es])
        if c == 0:
            r = 1.0 / l
            weights = [e * r for e in es]
        else:
            r = lam / l
            weights = [w - e * r for w, e in zip(weights, es)]
    o = None
    for w, vv in zip(weights, vs):
        part = jnp.dot(w.astype(BF16), vv, preferred_element_type=F32)
        o = part if o is None else o + part
    o = o * lax.rsqrt(jnp.mean(o * o, axis=1, keepdims=True) + SUBLN_EPSILON) * sub_ref[...] * (1.0 - lam_init)
    o_ref[...] = o.astype(o_ref.dtype)


def _attn_ctx_call(qkv, lam_p, subln, n_seq, length, d, dh, lam_init):
    nh = d // (2 * dh)
    tq = _divisor(length, 256, HALO)
    per_seq = length // tq
    est = 2 * (tq + 2 * length) * 2 * dh * 4 + 10 * tq * length * 4 + 4 * length * 2 * dh * 2
    return pl.pallas_call(
        functools.partial(_attn_kernel, latent=False, dh=dh, lam_init=lam_init),
        grid=(n_seq, nh, per_seq),
        in_specs=[pl.BlockSpec((4, dh), lambda b, h, i: (0, 0)),
                  pl.BlockSpec((1, 2 * dh), lambda b, h, i: (0, 0)),
                  pl.BlockSpec((tq, 2 * dh), lambda b, h, i: (b * per_seq + i, h)),
                  pl.BlockSpec((length, 2 * dh), lambda b, h, i: (b, nh + h)),
                  pl.BlockSpec((length, 2 * dh), lambda b, h, i: (b, 2 * nh + h))],
        out_specs=pl.BlockSpec((tq, 2 * dh), lambda b, h, i: (b * per_seq + i, h)),
        out_shape=jax.ShapeDtypeStruct((n_seq * length, d), BF16),
        compiler_params=_params(("parallel", "parallel", "arbitrary"), est),
        name="attn_context",
    )(lam_p, subln.reshape(1, 2 * dh), qkv, qkv, qkv)


def _attn_lat_call(qkv, kr, vb, ck, cv, cos, sin, lam_p, subln, tp, n_seq, length, d, dh, lam_init):
    nh = d // (2 * dh)
    lc = ck.shape[1]
    tq = _divisor(math.gcd(length, tp), 256, HALO)
    per_seq = length // tq
    est = (4 * (length + lc) * 2 * dh * 2 + 4 * lc * 2 * dh * 4 + 4 * tq * 2 * dh * 4
           + 5 * tq * (length + lc) * 4)
    return pl.pallas_call(
        functools.partial(_attn_kernel, latent=True, dh=dh, lam_init=lam_init),
        grid=(n_seq, nh, per_seq),
        in_specs=[pl.BlockSpec((4, dh), lambda b, h, i: (0, 0)),
                  pl.BlockSpec((1, 2 * dh), lambda b, h, i: (0, 0)),
                  pl.BlockSpec((tq, 2 * dh), lambda b, h, i: (tp // tq + b * per_seq + i, h)),
                  pl.BlockSpec((length, 2 * dh), lambda b, h, i: (b, h)),
                  pl.BlockSpec((length, 2 * dh), lambda b, h, i: (b, h)),
                  pl.BlockSpec((None, lc, 2 * dh), lambda b, h, i: (b, 0, h)),
                  pl.BlockSpec((None, lc, 2 * dh), lambda b, h, i: (b, 0, h)),
                  pl.BlockSpec((tq, dh), lambda b, h, i: (i, 0)),
                  pl.BlockSpec((tq, dh), lambda b, h, i: (i, 0))],
        out_specs=pl.BlockSpec((tq, 2 * dh), lambda b, h, i: (b * per_seq + i, h)),
        out_shape=jax.ShapeDtypeStruct((n_seq * length, d), BF16),
        compiler_params=_params(("parallel", "parallel", "arbitrary"), est),
        name="attn_latent",
    )(lam_p, subln.reshape(1, 2 * dh), qkv, kr, vb, ck, cv, cos, sin)


def _rope_tables(length, dh):
    half = dh // 2
    pos = jnp.arange(length)
    pos_r = (pos // GRID_COLS).astype(F32)
    pos_c = (pos % GRID_COLS).astype(F32)
    inv = jnp.power(ROPE_THETA, -jnp.arange(0, half, 2, dtype=F32) / half)
    ang_r = pos_r[:, None] * inv[None]
    ang_c = pos_c[:, None] * inv[None]
    cos = jnp.concatenate([jnp.cos(ang_r)] * 2 + [jnp.cos(ang_c)] * 2, axis=1)
    sin = jnp.concatenate([-jnp.sin(ang_r), jnp.sin(ang_r), -jnp.sin(ang_c), jnp.sin(ang_c)], axis=1)
    return cos, sin


def kernel(x_prompt, x_sample, cache_k, cache_v, c, c_ctx, mod_w, mod_b, ln1_g, ln1_b, ln2_g, ln2_b, pool_w, pool_scale, hyena_w_in, hyena_conv, hyena_ffn_w1, hyena_ffn_b1, hyena_ffn_w2, hyena_ffn_b2, hyena_ffn_w3, hyena_ffn_b3, hyena_freq, hyena_decay, hyena_skip, hyena_w_out, attn_w_qkv, attn_lambda, attn_subln, attn_w_out, ffn_w_up, ffn_conv, ffn_w_down):
    n_ctx, s_len, d = x_prompt.shape
    n_lat, ds, _ = x_sample.shape
    depth = mod_w.shape[0]
    tp, ts = n_ctx * s_len, n_lat * ds
    t = tp + ts
    dims = (tp, s_len, ds)
    alpha = (2 * depth) ** 0.25
    nh = cache_k.shape[3]
    dh = cache_k.shape[4] // 2
    lc = cache_k.shape[2]
    assert tp % ds == 0
    assert n_lat + 1 <= 8

    cond8 = jnp.zeros((8, d), F32).at[0].set(c_ctx).at[1:1 + n_lat].set(c)
    mods = _mods_call(cond8, mod_w, mod_b)
    mods4 = mods.reshape(depth, 8, 1, 6 * d)

    x = jnp.concatenate([x_prompt.reshape(tp, d), x_sample.reshape(ts, d)], axis=0)

    pos = jnp.concatenate([jnp.arange(tp) % s_len, jnp.arange(ts) % ds])
    last = jnp.concatenate([jnp.full((tp,), s_len - 1), jnp.full((ts,), ds - 1)])
    edge = jnp.stack([(pos != 0), (pos != last)], axis=1).astype(F32)

    f = ffn_w_down.shape[1]
    fp = -(-f // 512) * 512

    new_k = new_v = None
    h_in = None
    for i in range(depth):
        kind, j = i % N_MIXER_KINDS, i // N_MIXER_KINDS
        if kind == 0:
            x, h2 = _pool_call(x, mods4, i, pool_w[j].astype(BF16), pool_scale[j], ln1_g[i], ln1_b[i], dims, alpha)
        elif kind == 1:
            p = _up_conv_call(h_in, edge, hyena_w_in[j].astype(BF16), hyena_conv[j])
            z1 = jnp.zeros((t, d), F32)
            z2 = jnp.zeros((t, d), BF16)
            for length, n_seq, row0 in ((s_len, n_ctx, 0), (ds, n_lat, tp // ds)):
                hk, asum = _filter_call(length, hyena_ffn_w1[j], hyena_ffn_b1[j], hyena_ffn_w2[j], hyena_ffn_b2[j],
                                        hyena_ffn_w3[j], hyena_ffn_b3[j], hyena_freq[j], hyena_decay[j])
                tq = _divisor(length, 256, LANES)
                wf, wft = _dft_tables(length, tq)
                kre, kim = _kf_call(wf, hk, asum, tq)
                tabs = (wf, wft, tq)
                z1 = _dftconv_call(tabs, kre, kim, 0, hyena_skip[j, 0:1], p, 0, p, 1, row0, n_seq, z1, F32, d)
                z2 = _dftconv_call(tabs, kre, kim, 1, hyena_skip[j, 1:2], z1, 0, p, 2, row0, n_seq, z2, BF16, d)
            x, h2 = _down_ln_call(z2, hyena_w_out[j].astype(BF16), x, mods4, i, 2, i, 3, ln1_g[i], ln1_b[i],
                                  dims, alpha, "hyena_out")
        else:
            lam_init = 0.8 - 0.6 * math.exp(-0.3 * i)
            qkv = _up_plain_call(h_in, attn_w_qkv[j].astype(BF16))
            new_k = qkv[:tp, d:2 * d].reshape(n_ctx, s_len, nh, 2 * dh)
            new_v = qkv[:tp, 2 * d:3 * d].reshape(n_ctx, s_len, nh, 2 * dh)
            o_ctx = _attn_ctx_call(qkv, attn_lambda[j], attn_subln[j], n_ctx, s_len, d, dh, lam_init)
            cos, sin = _rope_tables(ds, dh)
            kr, vb = _rope_kv_call(qkv, cos, sin, tp, d, dh)
            o_lat = _attn_lat_call(qkv, kr, vb, cache_k[:, j].reshape(n_lat, lc, d), cache_v[:, j].reshape(n_lat, lc, d),
                                   cos, sin, attn_lambda[j], attn_subln[j], tp, n_lat, ds, d, dh, lam_init)
            o = jnp.concatenate([o_ctx, o_lat], axis=0)
            x, h2 = _down_ln_call(o, attn_w_out[j].astype(BF16), x, mods4, i, 2, i, 3, ln1_g[i], ln1_b[i],
                                  dims, alpha, "attn_out")
        wu = ffn_w_up[i]
        wg_p = jnp.pad(wu[:, :f], ((0, 0), (0, fp - f))).astype(BF16)
        wu_p = jnp.pad(wu[:, f:], ((0, 0), (0, fp - f))).astype(BF16)
        cg_p = jnp.pad(ffn_conv[i][:, :f], ((0, 0), (0, fp - f)))
        cu_p = jnp.pad(ffn_conv[i][:, f:], ((0, 0), (0, fp - f)))
        wd_p = jnp.pad(ffn_w_down[i], ((0, fp - f), (0, 0))).astype(BF16)
        a = _up_ffn_call(h2, edge, wg_p, wu_p, cg_p, cu_p)
        last_layer = i == depth - 1
        x, h_in = _down_ln_call(a, wd_p, x, mods4, i, 5, i + 1, None if last_layer else 0,
                                ln2_g[i], ln2_b[i], dims, alpha, "ffn_down")

    y_prompt = x[:tp].reshape(n_ctx, s_len, d)
    y_sample = x[tp:].reshape(n_lat, ds, d)
    return (y_prompt, y_sample, new_k[:, None], new_v[:, None])
```

```python
import functools
import math

import jax
import jax.numpy as jnp
from jax import lax
from jax.experimental import pallas as pl
from jax.experimental.pallas import tpu as pltpu

F32 = jnp.float32
BF16 = jnp.bfloat16

GRID_COLS = 64
POOL_WINDOW_SIZES = (2, 4, 8, 16)
ROPE_THETA = 10000.0
LN_EPSILON = 1e-6
SUBLN_EPSILON = 1e-5
N_MIXER_KINDS = 3

V7X_VMEM_BYTES = 64 * 1024 * 1024
V7X_VMEM_BUDGET = V7X_VMEM_BYTES - 6 * 1024 * 1024
SUBLANES_F32 = 8
SUBLANES_BF16 = 16
LANES = 128
LN_CHUNK_ROWS = 64


def _params(sem, est_bytes):
    limit = int(min(V7X_VMEM_BUDGET, max(32 * 1024 * 1024, est_bytes * 5 // 4)))
    return pltpu.CompilerParams(dimension_semantics=sem, vmem_limit_bytes=limit)


def _divisor(n, pref, mult):
    best = None
    d = mult
    while d <= min(n, pref):
        if n % d == 0:
            best = d
        d += mult
    assert best is not None, (n, pref, mult)
    return best


def _layer_norm(r, g, b):
    mu = jnp.mean(r, -1, keepdims=True)
    d = r - mu
    var = jnp.mean(d * d, -1, keepdims=True)
    return d * lax.rsqrt(var + LN_EPSILON) * g + b


def _mod_row(t0, tp, ds):
    return jnp.where(t0 < tp, 0, 1 + (t0 - tp) // ds)


def _mods_kernel(c_ref, w_ref, b_ref, o_ref):
    k = pl.program_id(2)

    @pl.when(k == 0)
    def _():
        o_ref[0] = jnp.broadcast_to(b_ref[0], o_ref.shape[1:])

    c = c_ref[...]
    s = (c * jax.nn.sigmoid(c)).astype(BF16)
    o_ref[0] += jnp.dot(s, w_ref[0].astype(BF16), preferred_element_type=F32)


def _mods_call(cond8, mod_w, mod_b):
    depth, d, n = mod_w.shape
    tk = _divisor(d, 2048, LANES)
    tn = _divisor(n, 1024, LANES)
    est = 2 * tk * tn * 4 + 4 * 8 * tn * 4 + 2 * 8 * tk * 4 + tk * tn * 2
    return pl.pallas_call(
        _mods_kernel,
        grid=(depth, n // tn, d // tk),
        in_specs=[pl.BlockSpec((8, tk), lambda l, j, k: (0, k)),
                  pl.BlockSpec((1, tk, tn), lambda l, j, k: (l, k, j)),
                  pl.BlockSpec((1, 1, tn), lambda l, j, k: (l, 0, j))],
        out_specs=pl.BlockSpec((1, 8, tn), lambda l, j, k: (l, 0, j)),
        out_shape=jax.ShapeDtypeStruct((depth, 8, n), F32),
        compiler_params=_params(("parallel", "parallel", "arbitrary"), est),
        name="mods",
    )(cond8, mod_w, mod_b.reshape(depth, 1, n))


def _pool_kernel(xm_ref, xp_ref, xn_ref, mod_ref, pw_ref, ps_ref, lg_ref, lb_ref, xo_ref, ho_ref, e_ref,
                 *, tm, tp, s_len, ds, d, alpha):
    t0 = pl.program_id(0) * tm
    is_ctx = t0 < tp
    seq_len = jnp.where(is_ctx, s_len, ds)
    pos0 = jnp.where(is_ctx, t0 % s_len, (t0 - tp) % ds)
    mods = mod_ref[...]
    sh1, sc1, g1 = mods[:, 0:d], mods[:, d:2 * d], mods[:, 2 * d:3 * d]
    sh2, sc2 = mods[:, 3 * d:4 * d], mods[:, 4 * d:5 * d]
    x = xm_ref[...]
    h = x * (1.0 + sc1) + sh1
    has_prev = pos0 > 0
    has_next = pos0 + tm < seq_len
    e_ref[0:8, :] = jnp.where(has_prev, xp_ref[...] * (1.0 + sc1) + sh1, 0.0)
    e_ref[8:8 + tm, :] = h
    e_ref[8 + tm:16 + tm, :] = jnp.where(has_next, xn_ref[...] * (1.0 + sc1) + sh1, 0.0)
    t = pos0 + lax.broadcasted_iota(jnp.int32, (tm, 1), 0)
    cg = d // len(POOL_WINDOW_SIZES)
    ys = []
    for g, w in enumerate(POOL_WINDOW_SIZES):
        c0 = g * cg
        acc = None
        for s in range(-(w // 2), w - w // 2):
            v = e_ref[8 + s:8 + s + tm, c0:c0 + cg]
            acc = v if acc is None else acc + v
        lo = jnp.maximum(t - w // 2, 0)
        hi = jnp.minimum(t + (w - w // 2), seq_len)
        inv = 1.0 / (hi - lo).astype(F32)
        p = acc * inv - h[:, c0:c0 + cg]
        ys.append(jnp.dot(p.astype(BF16), pw_ref[g], preferred_element_type=F32))
    y = jnp.concatenate(ys, axis=1) * ps_ref[...]
    xn = _layer_norm(alpha * x + g1 * y, lg_ref[...], lb_ref[...])
    xo_ref[...] = xn
    ho_ref[...] = (xn * (1.0 + sc2) + sh2).astype(BF16)


def _pool_call(x, mods4, layer, pw, ps, lg, lb, dims, alpha):
    t, d = x.shape
    tp, s_len, ds = dims
    tm = s_len
    assert ds % tm == 0 and tm % SUBLANES_BF16 == 0
    n8 = t // 8
    kern = functools.partial(_pool_kernel, tm=tm, tp=tp, s_len=s_len, ds=ds, d=d, alpha=alpha)
    g, cg, _ = pw.shape
    est = (2 * tm * d * 4 * 2 + 2 * tm * d * 2 + (tm + 16) * d * 4 + 2 * g * cg * cg * 2
           + 6 * tm * d * 4)
    return pl.pallas_call(
        kern,
        grid=(t // tm,),
        in_specs=[pl.BlockSpec((tm, d), lambda i: (i, 0)),
                  pl.BlockSpec((8, d), lambda i: (jnp.maximum(i * (tm // 8) - 1, 0), 0)),
                  pl.BlockSpec((8, d), lambda i: (jnp.minimum((i + 1) * (tm // 8), n8 - 1), 0)),
                  pl.BlockSpec((None, None, 1, 6 * d), lambda i: (layer, _mod_row(i * tm, tp, ds), 0, 0)),
                  pl.BlockSpec((g, cg, cg), lambda i: (0, 0, 0)),
                  pl.BlockSpec((1, d), lambda i: (0, 0)),
                  pl.BlockSpec((1, d), lambda i: (0, 0)),
                  pl.BlockSpec((1, d), lambda i: (0, 0))],
        out_specs=[pl.BlockSpec((tm, d), lambda i: (i, 0)),
                   pl.BlockSpec((tm, d), lambda i: (i, 0))],
        out_shape=[jax.ShapeDtypeStruct((t, d), F32), jax.ShapeDtypeStruct((t, d), BF16)],
        scratch_shapes=[pltpu.VMEM((tm + 16, d), F32)],
        compiler_params=_params(("parallel",), est),
        name="pool_mixer",
    )(x, x, x, mods4, pw, ps.reshape(1, d), lg.reshape(1, d), lb.reshape(1, d))


HALO = SUBLANES_BF16


def _fill_ext(ext_ref, a_ref, ap_ref, an_ref, tm):
    @pl.when(pl.program_id(1) == 0)
    def _():
        ext_ref[0:HALO, :] = ap_ref[...]
        ext_ref[HALO:HALO + tm, :] = a_ref[...]
        ext_ref[HALO + tm:2 * HALO + tm, :] = an_ref[...]


def _conv3(p, cw, not_first, not_last, tm):
    rows = p.shape[0]
    up = pltpu.roll(p, 1, 0)[HALO:HALO + tm]
    dn = pltpu.roll(p, rows - 1, 0)[HALO:HALO + tm]
    mid = p[HALO:HALO + tm]
    up = jnp.where(not_first > 0.0, up, 0.0)
    dn = jnp.where(not_last > 0.0, dn, 0.0)
    return cw[0:1] * up + cw[1:2] * mid + cw[2:3] * dn


def _up_ffn_kernel(a_ref, ap_ref, an_ref, edge_ref, wg_ref, wu_ref, cg_ref, cu_ref, o_ref, ext_ref, *, tm):
    _fill_ext(ext_ref, a_ref, ap_ref, an_ref, tm)
    e = edge_ref[...]
    nf, nl = e[:, 0:1], e[:, 1:2]
    a = ext_ref[...]
    g = _conv3(jnp.dot(a, wg_ref[...], preferred_element_type=F32), cg_ref[...], nf, nl, tm)
    u = _conv3(jnp.dot(a, wu_ref[...], preferred_element_type=F32), cu_ref[...], nf, nl, tm)
    o_ref[...] = (g * jax.nn.sigmoid(g) * u).astype(o_ref.dtype)


def _up_conv_kernel(a_ref, ap_ref, an_ref, edge_ref, w_ref, cw_ref, o_ref, ext_ref, *, tm):
    _fill_ext(ext_ref, a_ref, ap_ref, an_ref, tm)
    e = edge_ref[...]
    p = jnp.dot(ext_ref[...], w_ref[...], preferred_element_type=F32)
    o_ref[...] = _conv3(p, cw_ref[...], e[:, 0:1], e[:, 1:2], tm).astype(o_ref.dtype)


def _up_plain_kernel(a_ref, w_ref, o_ref):
    o_ref[...] = jnp.dot(a_ref[...], w_ref[...], preferred_element_type=F32).astype(o_ref.dtype)


def _halo_specs(tm, d, t):
    nh = t // HALO
    return [pl.BlockSpec((tm, d), lambda i, j: (i, 0)),
            pl.BlockSpec((HALO, d), lambda i, j: (jnp.maximum(i * (tm // HALO) - 1, 0), 0)),
            pl.BlockSpec((HALO, d), lambda i, j: (jnp.minimum((i + 1) * (tm // HALO), nh - 1), 0)),
            pl.BlockSpec((tm, 2), lambda i, j: (i, 0))]


def _up_ffn_call(h, edge, wg, wu, cg, cu):
    t, d = h.shape
    fp = wg.shape[1]
    tm = _divisor(t, 1024, HALO)
    tn = _divisor(fp, 512, LANES)
    est = (2 * tm * d * 2 + (tm + 2 * HALO) * d * 2 + 4 * d * tn * 2 + 2 * tm * tn * 2
           + 8 * (tm + 2 * HALO) * tn * 4)
    return pl.pallas_call(
        functools.partial(_up_ffn_kernel, tm=tm),
        grid=(t // tm, fp // tn),
        in_specs=_halo_specs(tm, d, t) + [
            pl.BlockSpec((d, tn), lambda i, j: (0, j)),
            pl.BlockSpec((d, tn), lambda i, j: (0, j)),
            pl.BlockSpec((3, tn), lambda i, j: (0, j)),
            pl.BlockSpec((3, tn), lambda i, j: (0, j))],
        out_specs=pl.BlockSpec((tm, tn), lambda i, j: (i, j)),
        out_shape=jax.ShapeDtypeStruct((t, fp), BF16),
        scratch_shapes=[pltpu.VMEM((tm + 2 * HALO, d), BF16)],
        compiler_params=_params(("parallel", "arbitrary"), est),
        name="ffn_up",
    )(h, h, h, edge, wg, wu, cg, cu)


def _up_conv_call(h, edge, w, cw):
    t, d = h.shape
    n = w.shape[1]
    tm = _divisor(t, 1024, HALO)
    tn = _divisor(n, 512, LANES)
    est = (2 * tm * d * 2 + (tm + 2 * HALO) * d * 2 + 2 * d * tn * 2 + 2 * tm * tn * 4
           + 5 * (tm + 2 * HALO) * tn * 4)
    return pl.pallas_call(
        functools.partial(_up_conv_kernel, tm=tm),
        grid=(t // tm, n // tn),
        in_specs=_halo_specs(tm, d, t) + [
            pl.BlockSpec((d, tn), lambda i, j: (0, j)),
            pl.BlockSpec((3, tn), lambda i, j: (0, j))],
        out_specs=pl.BlockSpec((tm, tn), lambda i, j: (i, j)),
        out_shape=jax.ShapeDtypeStruct((t, n), F32),
        scratch_shapes=[pltpu.VMEM((tm + 2 * HALO, d), BF16)],
        compiler_params=_params(("parallel", "arbitrary"), est),
        name="hyena_in",
    )(h, h, h, edge, w, cw)


def _up_plain_call(h, w):
    t, d = h.shape
    n = w.shape[1]
    tm = _divisor(t, 1024, HALO)
    tn = _divisor(n, 512, LANES)
    est = 2 * tm * d * 2 + 2 * d * tn * 2 + 2 * tm * tn * 4 + 2 * tm * tn * 4
    return pl.pallas_call(
        _up_plain_kernel,
        grid=(t // tm, n // tn),
        in_specs=[pl.BlockSpec((tm, d), lambda i, j: (i, 0)),
                  pl.BlockSpec((d, tn), lambda i, j: (0, j))],
        out_specs=pl.BlockSpec((tm, tn), lambda i, j: (i, j)),
        out_shape=jax.ShapeDtypeStruct((t, n), F32),
        compiler_params=_params(("parallel", "arbitrary"), est),
        name="attn_qkv",
    )(h, w)


def _down_ln_kernel(*refs, nj, tn, d, gate, alpha, nxt):
    if nxt is None:
        a_ref, w_ref, x_ref, mod_ref, lg_ref, lb_ref, xo_ref = refs
    else:
        a_ref, w_ref, x_ref, mod_ref, modn_ref, lg_ref, lb_ref, xo_ref, ho_ref = refs
    j = pl.program_id(1)
    col = pl.multiple_of(j * tn, tn)
    xo_ref[:, pl.ds(col, tn)] = jnp.dot(a_ref[...], w_ref[...], preferred_element_type=F32)

    @pl.when(j == nj - 1)
    def _():
        g = mod_ref[:, gate * d:(gate + 1) * d]
        tm = xo_ref.shape[0]
        ch = math.gcd(tm, LN_CHUNK_ROWS)

        def chunk(r, carry):
            rows = pl.ds(pl.multiple_of(r * ch, ch), ch)
            xn = _layer_norm(alpha * x_ref[rows, :] + g * xo_ref[rows, :], lg_ref[...], lb_ref[...])
            xo_ref[rows, :] = xn
            if nxt is not None:
                sh = modn_ref[:, nxt * d:(nxt + 1) * d]
                sc = modn_ref[:, (nxt + 1) * d:(nxt + 2) * d]
                ho_ref[rows, :] = (xn * (1.0 + sc) + sh).astype(BF16)
            return carry

        lax.fori_loop(0, tm // ch, chunk, 0)


def _down_ln_call(a, w, x, mods4, layer, gate, nxt_layer, nxt, lg, lb, dims, alpha, name):
    t, kdim = a.shape
    d = w.shape[1]
    tp, s_len, ds = dims
    tm = _divisor(math.gcd(tp, ds), 512, HALO)
    tn = _divisor(d, 512, LANES)
    nj = d // tn
    kern = functools.partial(_down_ln_kernel, nj=nj, tn=tn, d=d, gate=gate, alpha=alpha, nxt=nxt)
    once = pl.Buffered(1)
    mod_spec = lambda l: pl.BlockSpec((None, None, 1, 6 * d),
                                      lambda i, k: (l, _mod_row(i * tm, tp, ds), 0, 0))
    in_specs = [pl.BlockSpec((tm, kdim), lambda i, k: (i, 0), pipeline_mode=once),
                pl.BlockSpec((kdim, tn), lambda i, k: (0, k)),
                pl.BlockSpec((tm, d), lambda i, k: (i, 0), pipeline_mode=once),
                mod_spec(layer)]
    args = [a, w, x, mods4]
    if nxt is not None:
        in_specs.append(mod_spec(nxt_layer))
        args.append(mods4)
    in_specs += [pl.BlockSpec((1, d), lambda i, k: (0, 0)), pl.BlockSpec((1, d), lambda i, k: (0, 0))]
    args += [lg.reshape(1, d), lb.reshape(1, d)]
    out_specs = [pl.BlockSpec((tm, d), lambda i, k: (i, 0), pipeline_mode=once)]
    out_shape = [jax.ShapeDtypeStruct((t, d), F32)]
    if nxt is not None:
        out_specs.append(pl.BlockSpec((tm, d), lambda i, k: (i, 0), pipeline_mode=once))
        out_shape.append(jax.ShapeDtypeStruct((t, d), BF16))
    est = tm * kdim * 2 + 2 * kdim * tn * 2 + 2 * tm * d * 4 + tm * d * 2 + 2 * tm * tn * 4 + (4 << 20)
    res = pl.pallas_call(
        kern,
        grid=(t // tm, nj),
        in_specs=in_specs,
        out_specs=out_specs,
        out_shape=out_shape,
        compiler_params=_params(("parallel", "arbitrary"), est),
        name=name,
    )(*args)
    return (res[0], res[1]) if nxt is not None else (res[0], None)


def _dot_f32(a, b):
    return lax.dot_general(a, b, (((1,), (0,)), ((), ())), precision=lax.Precision.HIGHEST,
                           preferred_element_type=F32)


def _filter_kernel(z_ref, tu_ref, w1_ref, b1_ref, w2_ref, b2_ref, w3_ref, b3_ref, fr_ref, dec_ref,
                   hk_ref, as_ref, *, tl, n_col_tiles):
    jc = pl.program_id(0)
    i = pl.program_id(1)
    h = jnp.sin(fr_ref[0:1, :] * (_dot_f32(z_ref[...], w1_ref[...]) + b1_ref[...]))
    h = jnp.sin(fr_ref[1:2, :] * (_dot_f32(h, w2_ref[...]) + b2_ref[...]))
    h = _dot_f32(h, w3_ref[...]) + b3_ref[...]
    h = h * jnp.exp(-tu_ref[...] * jnp.abs(dec_ref[...]))
    row = i * tl + lax.broadcasted_iota(jnp.int32, (tl, 1), 0)
    drop = jnp.logical_and(row == 0, jc >= n_col_tiles // 2)
    h = jnp.where(drop, 0.0, h)
    hk_ref[...] = h.astype(BF16)
    part = jnp.sum(jnp.abs(h), axis=0, keepdims=True)

    @pl.when(i == 0)
    def _():
        as_ref[...] = part

    @pl.when(i > 0)
    def _():
        as_ref[...] += part


def _filter_call(length, w1, b1, w2, b2, w3, b3, freq, decay):
    emb, width = w1.shape
    n = w3.shape[1]
    t = jnp.arange(length, dtype=F32)
    t_unit = t / max(length - 1, 1)
    n_bands = (emb - 1) // 2
    bands = jnp.linspace(1e-4, n_bands - 1, n_bands, dtype=F32)
    ang = (2.0 * math.pi / length) * t[:, None] * bands[None, :]
    z = jnp.concatenate([t_unit[:, None], jnp.cos(ang), -jnp.sin(ang)], -1)
    z = jnp.pad(z, ((0, 0), (0, LANES - emb)))
    w1p = jnp.pad(w1, ((0, LANES - emb), (0, 0)))
    tl = _divisor(length, 256, 8)
    tn = _divisor(n // 2, 2048, LANES)
    nct = n // tn
    est = 2 * tl * tn * 2 + 6 * tl * tn * 4 + 4 * width * tn * 4
    return pl.pallas_call(
        functools.partial(_filter_kernel, tl=tl, n_col_tiles=nct),
        grid=(nct, length // tl),
        in_specs=[pl.BlockSpec((tl, LANES), lambda j, i: (i, 0)),
                  pl.BlockSpec((tl, 1), lambda j, i: (i, 0)),
                  pl.BlockSpec((LANES, width), lambda j, i: (0, 0)),
                  pl.BlockSpec((1, width), lambda j, i: (0, 0)),
                  pl.BlockSpec((width, width), lambda j, i: (0, 0)),
                  pl.BlockSpec((1, width), lambda j, i: (0, 0)),
                  pl.BlockSpec((width, tn), lambda j, i: (0, j)),
                  pl.BlockSpec((1, tn), lambda j, i: (0, j)),
                  pl.BlockSpec((2, width), lambda j, i: (0, 0)),
                  pl.BlockSpec((1, tn), lambda j, i: (0, j))],
        out_specs=[pl.BlockSpec((tl, tn), lambda j, i: (i, j)),
                   pl.BlockSpec((1, tn), lambda j, i: (0, j))],
        out_shape=[jax.ShapeDtypeStruct((length, n), BF16), jax.ShapeDtypeStruct((1, n), F32)],
        compiler_params=_params(("parallel", "arbitrary"), est),
        name="hyena_filter",
    )(z, t_unit[:, None], w1p, b1.reshape(1, width), w2, b2.reshape(1, width), w3, b3.reshape(1, n),
      freq, decay.reshape(1, n))


def _dft_tables(length, tq):
    n2 = 2 * length
    nq = length // tq
    n = jnp.arange(length, dtype=jnp.int32)
    pa = (jnp.arange(nq, dtype=jnp.int32)[:, None] * tq * n[None, :]) % n2
    pb = (jnp.arange(tq, dtype=jnp.int32)[:, None] * n[None, :]) % n2
    a = pa.astype(F32) * (2.0 * math.pi / n2)
    b = pb.astype(F32) * (2.0 * math.pi / n2)
    ca, sa = jnp.cos(a)[:, None, :], jnp.sin(a)[:, None, :]
    cb, sb = jnp.cos(b)[None, :, :], jnp.sin(b)[None, :, :]
    cos_rows = ca * cb - sa * sb
    sin_rows = -(sa * cb + ca * sb)
    nyq = jnp.where(n % 2 == 0, 1.0, -1.0)[None, None, :]
    first = jnp.logical_and(lax.broadcasted_iota(jnp.int32, (nq, tq, 1), 0) == 0,
                            lax.broadcasted_iota(jnp.int32, (nq, tq, 1), 1) == 0)
    sin_rows = jnp.where(first, nyq, sin_rows)
    wf = jnp.concatenate([cos_rows, sin_rows], axis=1).reshape(2 * length, length).astype(BF16)
    return wf, wf.T


def _kf_kernel(wf_ref, hf_ref, hb_ref, af_ref, ab_ref, kre_ref, kim_ref, *, tq, n2):
    q = pl.program_id(1)
    ff = jnp.dot(wf_ref[...], hf_ref[...], preferred_element_type=F32)
    fb = jnp.dot(wf_ref[...], hb_ref[...], preferred_element_type=F32)
    inv = 1.0 / (af_ref[...] + ab_ref[...] + 1e-6)
    f0 = (q * tq + lax.broadcasted_iota(jnp.int32, (tq, 1), 0)) == 0
    w = jnp.where(f0, 1.0 / n2, 2.0 / n2) * inv
    kre_ref[...] = (ff[:tq] + fb[:tq]) * w
    kim_ref[...] = jnp.where(f0, ff[tq:] + fb[tq:], ff[tq:] - fb[tq:]) * w


def _kf_call(wf, hk, asum, tq):
    length = hk.shape[0]
    n = hk.shape[1] // 2
    dt = _divisor(n, 512, LANES)
    nq = length // tq
    nb = n // dt
    est = 2 * 2 * tq * length * 2 + 4 * length * dt * 2 + 4 * tq * dt * 4 + 6 * 2 * tq * dt * 4
    return pl.pallas_call(
        functools.partial(_kf_kernel, tq=tq, n2=2 * length),
        grid=(nb, nq),
        in_specs=[pl.BlockSpec((2 * tq, length), lambda j, q: (q, 0)),
                  pl.BlockSpec((length, dt), lambda j, q: (0, j)),
                  pl.BlockSpec((length, dt), lambda j, q: (0, nb + j)),
                  pl.BlockSpec((1, dt), lambda j, q: (0, j)),
                  pl.BlockSpec((1, dt), lambda j, q: (0, nb + j))],
        out_specs=[pl.BlockSpec((tq, dt), lambda j, q: (q, j)),
                   pl.BlockSpec((tq, dt), lambda j, q: (q, j))],
        out_shape=[jax.ShapeDtypeStruct((length, n), F32), jax.ShapeDtypeStruct((length, n), F32)],
        compiler_params=_params(("parallel", "arbitrary"), est),
        name="hyena_filter_dft",
    )(wf, hk, hk, asum, asum)


def _dftconv_kernel(*refs, nb, nq, tq, dt, own_acc):
    wf_ref, wft_ref, kre_ref, kim_ref, skip_ref = refs[:5]
    u_refs = refs[5:5 + nb]
    g_refs = refs[5 + nb:5 + 2 * nb]
    o_ref = refs[5 + 2 * nb]
    ub_ref = refs[6 + 2 * nb]
    acc_ref = refs[7 + 2 * nb] if own_acc else o_ref
    q = pl.program_id(2)
    half = wft_ref.shape[0] // 2

    @pl.when(q == 0)
    def _():
        for b in range(nb):
            u = u_refs[b][...]
            ub_ref[:, b * dt:(b + 1) * dt] = u.astype(BF16)
            acc_ref[b] = u * skip_ref[...]

    uf = jnp.dot(wf_ref[...], ub_ref[...], preferred_element_type=F32)
    ure, uim = uf[:tq], uf[tq:]
    kre = jnp.concatenate([kre_ref[...]] * nb, axis=1)
    kim = jnp.concatenate([kim_ref[...]] * nb, axis=1)
    f0 = (q * tq + lax.broadcasted_iota(jnp.int32, (tq, 1), 0)) == 0
    yre = ure * kre - jnp.where(f0, 0.0, uim * kim)
    yim = jnp.where(f0, uim * kim, ure * kim + uim * kre)
    y = jnp.concatenate([yre, yim], axis=0).astype(BF16)
    for hh in range(2):
        rows = slice(hh * half, (hh + 1) * half)
        res = jnp.dot(wft_ref[rows, :], y, preferred_element_type=F32)
        for b in range(nb):
            acc_ref[b, rows, :] += res[:, b * dt:(b + 1) * dt]

    @pl.when(q == nq - 1)
    def _():
        for b in range(nb):
            o_ref[b] = (g_refs[b][...] * acc_ref[b]).astype(o_ref.dtype)


def _dftconv_call(tabs, kre, kim, korder, skip, u3, urow0, ucol, g3, grow0, gcol, n_seq, out_dtype, d):
    wf, wft, tq = tabs
    length = wft.shape[0]
    nq = length // tq
    long_seq = length > 1024
    dt = _divisor(d, 256 if long_seq else 512, LANES)
    nb = _divisor(n_seq, 2 if long_seq else 4, 1)
    nd = d // dt
    own_acc = out_dtype != F32
    once = pl.Buffered(1)
    blk = length * dt
    est = (4 * 2 * tq * length * 2 + 2 * nb * blk * 4 + nb * blk * (4 if own_acc else 0)
           + nb * blk * jnp.dtype(out_dtype).itemsize + nb * blk * 2 + nb * blk * 2 + 8 * tq * nb * dt * 4)
    kern = functools.partial(_dftconv_kernel, nb=nb, nq=nq, tq=tq, dt=dt, own_acc=own_acc)
    seq_specs = lambda row0, col: [
        pl.BlockSpec((None, length, dt), lambda s, j, q, b=b: (row0 + s * nb + b, 0, col * nd + j),
                     pipeline_mode=once) for b in range(nb)]
    scratch = [pltpu.VMEM((length, nb * dt), BF16)]
    if own_acc:
        scratch.append(pltpu.VMEM((nb, length, dt), F32))
    return pl.pallas_call(
        kern,
        grid=(n_seq // nb, nd, nq),
        in_specs=[pl.BlockSpec((2 * tq, length), lambda s, j, q: (q, 0)),
                  pl.BlockSpec((length, 2 * tq), lambda s, j, q: (0, q)),
                  pl.BlockSpec((tq, dt), lambda s, j, q: (q, korder * nd + j)),
                  pl.BlockSpec((tq, dt), lambda s, j, q: (q, korder * nd + j)),
                  pl.BlockSpec((1, dt), lambda s, j, q: (0, j))]
                 + seq_specs(urow0, ucol) + seq_specs(grow0, gcol),
        out_specs=pl.BlockSpec((nb, length, dt), lambda s, j, q: (s, 0, j), pipeline_mode=once),
        out_shape=jax.ShapeDtypeStruct((n_seq, length, d), out_dtype),
        scratch_shapes=scratch,
        compiler_params=_params(("parallel", "parallel", "arbitrary"), est),
        name="hyena_conv",
    )(wf, wft, kre, kim, skip, *([u3] * nb), *([g3] * nb))


def _rope(x, cos, sin, dh):
    outs = []
    for c in range(2):
        xc = x[:, c * dh:(c + 1) * dh]
        lane = lax.broadcasted_iota(jnp.int32, xc.shape, 1)
        first_half = (lane & 32) == 0
        sw = jnp.where(first_half, pltpu.roll(xc, dh - 32, 1), pltpu.roll(xc, 32, 1))
        outs.append(xc * cos + sw * sin)
    return jnp.concatenate(outs, axis=1)


def _rope_kv_kernel(k_ref, v_ref, cos_ref, sin_ref, ko_ref, vo_ref, *, dh):
    ko_ref[...] = _rope(k_ref[...], cos_ref[...], sin_ref[...], dh).astype(BF16)
    vo_ref[...] = v_ref[...].astype(BF16)


def _rope_kv_call(qkv, cos, sin, tp, d, dh):
    t = qkv.shape[0]
    ts = t - tp
    length = cos.shape[0]
    tr = _divisor(math.gcd(length, tp), 512, HALO)
    nh = d // (2 * dh)
    per_seq = length // tr
    est = 8 * tr * 2 * dh * 4 + 4 * tr * dh * 4
    return pl.pallas_call(
        functools.partial(_rope_kv_kernel, dh=dh),
        grid=(ts // tr, nh),
        in_specs=[pl.BlockSpec((tr, 2 * dh), lambda i, h: (tp // tr + i, nh + h)),
                  pl.BlockSpec((tr, 2 * dh), lambda i, h: (tp // tr + i, 2 * nh + h)),
                  pl.BlockSpec((tr, dh), lambda i, h: (i % per_seq, 0)),
                  pl.BlockSpec((tr, dh), lambda i, h: (i % per_seq, 0))],
        out_specs=[pl.BlockSpec((tr, 2 * dh), lambda i, h: (i, h)),
                   pl.BlockSpec((tr, 2 * dh), lambda i, h: (i, h))],
        out_shape=[jax.ShapeDtypeStruct((ts, d), BF16), jax.ShapeDtypeStruct((ts, d), BF16)],
        compiler_params=_params(("parallel", "parallel"), est),
        name="attn_rope_kv",
    )(qkv, qkv, cos, sin)


def _attn_kernel(*refs, latent, dh, lam_init):
    if latent:
        lam_ref, sub_ref, q_ref, k_ref, v_ref, ck_ref, cv_ref, cos_ref, sin_ref, o_ref = refs
    else:
        lam_ref, sub_ref, q_ref, k_ref, v_ref, o_ref = refs
    scale = dh ** -0.5
    lp = lam_ref[...]
    lam = (jnp.exp(jnp.sum(lp[0:1] * lp[1:2], axis=1, keepdims=True))
           - jnp.exp(jnp.sum(lp[2:3] * lp[3:4], axis=1, keepdims=True)) + lam_init)
    q = q_ref[...]
    if latent:
        q = _rope(q, cos_ref[...], sin_ref[...], dh)
    qb = q.astype(BF16)
    ks = [k_ref[...].astype(BF16)]
    vs = [v_ref[...].astype(BF16)]
    if latent:
        ks.append(ck_ref[...].astype(BF16))
        vs.append(cv_ref[...].astype(BF16))
    nt = (((1,), (1,)), ((), ()))
    c2 = scale * math.log2(math.e)
    o = None
    for c in range(2):
        qc = qb[:, c * dh:(c + 1) * dh]
        ss = [lax.dot_general(qc, kk[:, c * dh:(c + 1) * dh], nt, preferred_element_type=F32) for kk in ks]
        m = functools.reduce(jnp.maximum, [jnp.max(s, axis=1, keepdims=True) for s in ss])
        es = [jnp.exp2((s - m) * c2) for s in ss]
        l = functools.reduce(lambda a, b: a + b, [jnp.sum(e, axis=1, keepdims=True) for e in es])
        pv = None
        for e, vv in zip(es, vs):
            part = jnp.dot(e.astype(BF16), vv, preferred_element_type=F32)
            pv = part if pv is None else pv + part
        o = pv * (1.0 / l) if c == 0 else o - pv * (lam / l)
    o = o * lax.rsqrt(jnp.mean(o * o, axis=1, keepdims=True) + SUBLN_EPSILON) * sub_ref[...] * (1.0 - lam_init)
    o_ref[...] = o.astype(o_ref.dtype)


def _attn_ctx_call(qkv, lam_p, subln, n_seq, length, d, dh, lam_init):
    nh = d // (2 * dh)
    tq = _divisor(length, 256, HALO)
    per_seq = length // tq
    est = 2 * (tq + 2 * length) * 2 * dh * 4 + 10 * tq * length * 4 + 4 * length * 2 * dh * 2
    return pl.pallas_call(
        functools.partial(_attn_kernel, latent=False, dh=dh, lam_init=lam_init),
        grid=(n_seq, nh, per_seq),
        in_specs=[pl.BlockSpec((4, dh), lambda b, h, i: (0, 0)),
                  pl.BlockSpec((1, 2 * dh), lambda b, h, i: (0, 0)),
                  pl.BlockSpec((tq, 2 * dh), lambda b, h, i: (b * per_seq + i, h)),
                  pl.BlockSpec((length, 2 * dh), lambda b, h, i: (b, nh + h)),
                  pl.BlockSpec((length, 2 * dh), lambda b, h, i: (b, 2 * nh + h))],
        out_specs=pl.BlockSpec((tq, 2 * dh), lambda b, h, i: (b * per_seq + i, h)),
        out_shape=jax.ShapeDtypeStruct((n_seq * length, d), BF16),
        compiler_params=_params(("parallel", "parallel", "arbitrary"), est),
        name="attn_context",
    )(lam_p, subln.reshape(1, 2 * dh), qkv, qkv, qkv)


def _attn_lat_call(qkv, kr, vb, ck, cv, cos, sin, lam_p, subln, tp, n_seq, length, d, dh, lam_init):
    nh = d // (2 * dh)
    lc = ck.shape[1]
    tq = _divisor(math.gcd(length, tp), 256, HALO)
    per_seq = length // tq
    est = (4 * (length + lc) * 2 * dh * 2 + 4 * lc * 2 * dh * 4 + 4 * tq * 2 * dh * 4
           + 5 * tq * (length + lc) * 4)
    return pl.pallas_call(
        functools.partial(_attn_kernel, latent=True, dh=dh, lam_init=lam_init),
        grid=(n_seq, nh, per_seq),
        in_specs=[pl.BlockSpec((4, dh), lambda b, h, i: (0, 0)),
                  pl.BlockSpec((1, 2 * dh), lambda b, h, i: (0, 0)),
                  pl.BlockSpec((tq, 2 * dh), lambda b, h, i: (tp // tq + b * per_seq + i, h)),
                  pl.BlockSpec((length, 2 * dh), lambda b, h, i: (b, h)),
                  pl.BlockSpec((length, 2 * dh), lambda b, h, i: (b, h)),
                  pl.BlockSpec((None, lc, 2 * dh), lambda b, h, i: (b, 0, h)),
                  pl.BlockSpec((None, lc, 2 * dh), lambda b, h, i: (b, 0, h)),
                  pl.BlockSpec((tq, dh), lambda b, h, i: (i, 0)),
                  pl.BlockSpec((tq, dh), lambda b, h, i: (i, 0))],
        out_specs=pl.BlockSpec((tq, 2 * dh), lambda b, h, i: (b * per_seq + i, h)),
        out_shape=jax.ShapeDtypeStruct((n_seq * length, d), BF16),
        compiler_params=_params(("parallel", "parallel", "arbitrary"), est),
        name="attn_latent",
    )(lam_p, subln.reshape(1, 2 * dh), qkv, kr, vb, ck, cv, cos, sin)


def _rope_tables(length, dh):
    half = dh // 2
    pos = jnp.arange(length)
    pos_r = (pos // GRID_COLS).astype(F32)
    pos_c = (pos % GRID_COLS).astype(F32)
    inv = jnp.power(ROPE_THETA, -jnp.arange(0, half, 2, dtype=F32) / half)
    ang_r = pos_r[:, None] * inv[None]
    ang_c = pos_c[:, None] * inv[None]
    cos = jnp.concatenate([jnp.cos(ang_r)] * 2 + [jnp.cos(ang_c)] * 2, axis=1)
    sin = jnp.concatenate([-jnp.sin(ang_r), jnp.sin(ang_r), -jnp.sin(ang_c), jnp.sin(ang_c)], axis=1)
    return cos, sin


def kernel(x_prompt, x_sample, cache_k, cache_v, c, c_ctx, mod_w, mod_b, ln1_g, ln1_b, ln2_g, ln2_b, pool_w, pool_scale, hyena_w_in, hyena_conv, hyena_ffn_w1, hyena_ffn_b1, hyena_ffn_w2, hyena_ffn_b2, hyena_ffn_w3, hyena_ffn_b3, hyena_freq, hyena_decay, hyena_skip, hyena_w_out, attn_w_qkv, attn_lambda, attn_subln, attn_w_out, ffn_w_up, ffn_conv, ffn_w_down):
    n_ctx, s_len, d = x_prompt.shape
    n_lat, ds, _ = x_sample.shape
    depth = mod_w.shape[0]
    tp, ts = n_ctx * s_len, n_lat * ds
    t = tp + ts
    dims = (tp, s_len, ds)
    alpha = (2 * depth) ** 0.25
    nh = cache_k.shape[3]
    dh = cache_k.shape[4] // 2
    lc = cache_k.shape[2]
    assert tp % ds == 0
    assert n_lat + 1 <= 8

    cond8 = jnp.zeros((8, d), F32).at[0].set(c_ctx).at[1:1 + n_lat].set(c)
    mods = _mods_call(cond8, mod_w, mod_b)
    mods4 = mods.reshape(depth, 8, 1, 6 * d)

    x = jnp.concatenate([x_prompt.reshape(tp, d), x_sample.reshape(ts, d)], axis=0)

    pos = jnp.concatenate([jnp.arange(tp) % s_len, jnp.arange(ts) % ds])
    last = jnp.concatenate([jnp.full((tp,), s_len - 1), jnp.full((ts,), ds - 1)])
    edge = jnp.stack([(pos != 0), (pos != last)], axis=1).astype(F32)

    f = ffn_w_down.shape[1]
    fp = -(-f // 512) * 512

    new_k = new_v = None
    h_in = None
    for i in range(depth):
        kind, j = i % N_MIXER_KINDS, i // N_MIXER_KINDS
        if kind == 0:
            x, h2 = _pool_call(x, mods4, i, pool_w[j].astype(BF16), pool_scale[j], ln1_g[i], ln1_b[i], dims, alpha)
        elif kind == 1:
            p = _up_conv_call(h_in, edge, hyena_w_in[j].astype(BF16), hyena_conv[j])
            z2s = []
            for length, n_seq, row0 in ((s_len, n_ctx, 0), (ds, n_lat, tp // ds)):
                hk, asum = _filter_call(length, hyena_ffn_w1[j], hyena_ffn_b1[j], hyena_ffn_w2[j], hyena_ffn_b2[j],
                                        hyena_ffn_w3[j], hyena_ffn_b3[j], hyena_freq[j], hyena_decay[j])
                tq = _divisor(length, 256, LANES)
                wf, wft = _dft_tables(length, tq)
                kre, kim = _kf_call(wf, hk, asum, tq)
                tabs = (wf, wft, tq)
                p3 = p.reshape(t // length, length, 3 * d)
                z1 = _dftconv_call(tabs, kre, kim, 0, hyena_skip[j, 0:1], p3, row0, 0, p3, row0, 1, n_seq, F32, d)
                z2 = _dftconv_call(tabs, kre, kim, 1, hyena_skip[j, 1:2], z1, 0, 0, p3, row0, 2, n_seq, BF16, d)
                z2s.append(z2.reshape(n_seq * length, d))
            z2 = jnp.concatenate(z2s, axis=0)
            x, h2 = _down_ln_call(z2, hyena_w_out[j].astype(BF16), x, mods4, i, 2, i, 3, ln1_g[i], ln1_b[i],
                                  dims, alpha, "hyena_out")
        else:
            lam_init = 0.8 - 0.6 * math.exp(-0.3 * i)
            qkv = _up_plain_call(h_in, attn_w_qkv[j].astype(BF16))
            new_k = qkv[:tp, d:2 * d].reshape(n_ctx, s_len, nh, 2 * dh)
            new_v = qkv[:tp, 2 * d:3 * d].reshape(n_ctx, s_len, nh, 2 * dh)
            o_ctx = _attn_ctx_call(qkv, attn_lambda[j], attn_subln[j], n_ctx, s_len, d, dh, lam_init)
            cos, sin = _rope_tables(ds, dh)
            kr, vb = _rope_kv_call(qkv, cos, sin, tp, d, dh)
            o_lat = _attn_lat_call(qkv, kr, vb, cache_k[:, j].reshape(n_lat, lc, d), cache_v[:, j].reshape(n_lat, lc, d),
                                   cos, sin, attn_lambda[j], attn_subln[j], tp, n_lat, ds, d, dh, lam_init)
            o = jnp.concatenate([o_ctx, o_lat], axis=0)
            x, h2 = _down_ln_call(o, attn_w_out[j].astype(BF16), x, mods4, i, 2, i, 3, ln1_g[i], ln1_b[i],
                                  dims, alpha, "attn_out")
        wu = ffn_w_up[i]
        wg_p = jnp.pad(wu[:, :f], ((0, 0), (0, fp - f))).astype(BF16)
        wu_p = jnp.pad(wu[:, f:], ((0, 0), (0, fp - f))).astype(BF16)
        cg_p = jnp.pad(ffn_conv[i][:, :f], ((0, 0), (0, fp - f)))
        cu_p = jnp.pad(ffn_conv[i][:, f:], ((0, 0), (0, fp - f)))
        wd_p = jnp.pad(ffn_w_down[i], ((0, fp - f), (0, 0))).astype(BF16)
        a = _up_ffn_call(h2, edge, wg_p, wu_p, cg_p, cu_p)
        last_layer = i == depth - 1
        x, h_in = _down_ln_call(a, wd_p, x, mods4, i, 5, i + 1, None if last_layer else 0,
                                ln2_g[i], ln2_b[i], dims, alpha, "ffn_down")

    y_prompt = x[:tp].reshape(n_ctx, s_len, d)
    y_sample = x[tp:].reshape(n_lat, ds, d)
    return (y_prompt, y_sample, new_k[:, None], new_v[:, None])
```

```python
import functools
import math

import jax
import jax.numpy as jnp
from jax import lax
from jax.experimental import pallas as pl
from jax.experimental.pallas import tpu as pltpu

F32 = jnp.float32
BF16 = jnp.bfloat16

GRID_COLS = 64
POOL_WINDOW_SIZES = (2, 4, 8, 16)
ROPE_THETA = 10000.0
LN_EPSILON = 1e-6
SUBLN_EPSILON = 1e-5
N_MIXER_KINDS = 3

V7X_VMEM_BYTES = 64 * 1024 * 1024
V7X_VMEM_BUDGET = V7X_VMEM_BYTES - 6 * 1024 * 1024
SUBLANES_F32 = 8
SUBLANES_BF16 = 16
LANES = 128
KV_CHUNK = 512
LN_CHUNK_ROWS = 64


def _params(sem, est_bytes):
    limit = int(min(V7X_VMEM_BUDGET, max(32 * 1024 * 1024, est_bytes * 5 // 4)))
    return pltpu.CompilerParams(dimension_semantics=sem, vmem_limit_bytes=limit)


def _divisor(n, pref, mult):
    best = None
    d = mult
    while d <= min(n, pref):
        if n % d == 0:
            best = d
        d += mult
    assert best is not None, (n, pref, mult)
    return best


def _layer_norm(r, g, b):
    mu = jnp.mean(r, -1, keepdims=True)
    d = r - mu
    var = jnp.mean(d * d, -1, keepdims=True)
    return d * lax.rsqrt(var + LN_EPSILON) * g + b


def _mod_row(t0, tp, ds):
    return jnp.where(t0 < tp, 0, 1 + (t0 - tp) // ds)


def _mods_kernel(c_ref, w_ref, b_ref, o_ref):
    k = pl.program_id(2)

    @pl.when(k == 0)
    def _():
        o_ref[0] = jnp.broadcast_to(b_ref[0], o_ref.shape[1:])

    c = c_ref[...]
    s = (c * jax.nn.sigmoid(c)).astype(BF16)
    o_ref[0] += jnp.dot(s, w_ref[0].astype(BF16), preferred_element_type=F32)


def _mods_call(cond8, mod_w, mod_b):
    depth, d, n = mod_w.shape
    tk = _divisor(d, 2048, LANES)
    tn = _divisor(n, 1024, LANES)
    est = 2 * tk * tn * 4 + 4 * 8 * tn * 4 + 2 * 8 * tk * 4 + tk * tn * 2
    return pl.pallas_call(
        _mods_kernel,
        grid=(depth, n // tn, d // tk),
        in_specs=[pl.BlockSpec((8, tk), lambda l, j, k: (0, k)),
                  pl.BlockSpec((1, tk, tn), lambda l, j, k: (l, k, j)),
                  pl.BlockSpec((1, 1, tn), lambda l, j, k: (l, 0, j))],
        out_specs=pl.BlockSpec((1, 8, tn), lambda l, j, k: (l, 0, j)),
        out_shape=jax.ShapeDtypeStruct((depth, 8, n), F32),
        compiler_params=_params(("parallel", "parallel", "arbitrary"), est),
        name="mods",
    )(cond8, mod_w, mod_b.reshape(depth, 1, n))


def _pool_kernel(xm_ref, xp_ref, xn_ref, mod_ref, pw_ref, ps_ref, lg_ref, lb_ref, xo_ref, ho_ref, e_ref,
                 *, tm, tp, s_len, ds, d, alpha):
    t0 = pl.program_id(0) * tm
    is_ctx = t0 < tp
    seq_len = jnp.where(is_ctx, s_len, ds)
    pos0 = jnp.where(is_ctx, t0 % s_len, (t0 - tp) % ds)
    mods = mod_ref[...]
    sh1, sc1, g1 = mods[:, 0:d], mods[:, d:2 * d], mods[:, 2 * d:3 * d]
    sh2, sc2 = mods[:, 3 * d:4 * d], mods[:, 4 * d:5 * d]
    x = xm_ref[...]
    h = x * (1.0 + sc1) + sh1
    has_prev = pos0 > 0
    has_next = pos0 + tm < seq_len
    e_ref[0:8, :] = jnp.where(has_prev, xp_ref[...] * (1.0 + sc1) + sh1, 0.0)
    e_ref[8:8 + tm, :] = h
    e_ref[8 + tm:16 + tm, :] = jnp.where(has_next, xn_ref[...] * (1.0 + sc1) + sh1, 0.0)
    t = pos0 + lax.broadcasted_iota(jnp.int32, (tm, 1), 0)
    cg = d // len(POOL_WINDOW_SIZES)
    ys = []
    for g, w in enumerate(POOL_WINDOW_SIZES):
        c0 = g * cg
        acc = None
        for s in range(-(w // 2), w - w // 2):
            v = e_ref[8 + s:8 + s + tm, c0:c0 + cg]
            acc = v if acc is None else acc + v
        lo = jnp.maximum(t - w // 2, 0)
        hi = jnp.minimum(t + (w - w // 2), seq_len)
        inv = 1.0 / (hi - lo).astype(F32)
        p = acc * inv - h[:, c0:c0 + cg]
        ys.append(jnp.dot(p.astype(BF16), pw_ref[g], preferred_element_type=F32))
    y = jnp.concatenate(ys, axis=1) * ps_ref[...]
    xn = _layer_norm(alpha * x + g1 * y, lg_ref[...], lb_ref[...])
    xo_ref[...] = xn
    ho_ref[...] = (xn * (1.0 + sc2) + sh2).astype(BF16)


def _pool_call(x, mods4, layer, pw, ps, lg, lb, dims, alpha):
    t, d = x.shape
    tp, s_len, ds = dims
    tm = s_len
    assert ds % tm == 0 and tm % SUBLANES_BF16 == 0
    n8 = t // 8
    kern = functools.partial(_pool_kernel, tm=tm, tp=tp, s_len=s_len, ds=ds, d=d, alpha=alpha)
    g, cg, _ = pw.shape
    est = (2 * tm * d * 4 * 2 + 2 * tm * d * 2 + (tm + 16) * d * 4 + 2 * g * cg * cg * 2
           + 6 * tm * d * 4)
    return pl.pallas_call(
        kern,
        grid=(t // tm,),
        in_specs=[pl.BlockSpec((tm, d), lambda i: (i, 0)),
                  pl.BlockSpec((8, d), lambda i: (jnp.maximum(i * (tm // 8) - 1, 0), 0)),
                  pl.BlockSpec((8, d), lambda i: (jnp.minimum((i + 1) * (tm // 8), n8 - 1), 0)),
                  pl.BlockSpec((None, None, 1, 6 * d), lambda i: (layer, _mod_row(i * tm, tp, ds), 0, 0)),
                  pl.BlockSpec((g, cg, cg), lambda i: (0, 0, 0)),
                  pl.BlockSpec((1, d), lambda i: (0, 0)),
                  pl.BlockSpec((1, d), lambda i: (0, 0)),
                  pl.BlockSpec((1, d), lambda i: (0, 0))],
        out_specs=[pl.BlockSpec((tm, d), lambda i: (i, 0)),
                   pl.BlockSpec((tm, d), lambda i: (i, 0))],
        out_shape=[jax.ShapeDtypeStruct((t, d), F32), jax.ShapeDtypeStruct((t, d), BF16)],
        scratch_shapes=[pltpu.VMEM((tm + 16, d), F32)],
        compiler_params=_params(("parallel",), est),
        name="pool_mixer",
    )(x, x, x, mods4, pw, ps.reshape(1, d), lg.reshape(1, d), lb.reshape(1, d))


HALO = SUBLANES_BF16


def _fill_ext(ext_ref, a_ref, ap_ref, an_ref, tm):
    @pl.when(pl.program_id(1) == 0)
    def _():
        ext_ref[0:HALO, :] = ap_ref[...]
        ext_ref[HALO:HALO + tm, :] = a_ref[...]
        ext_ref[HALO + tm:2 * HALO + tm, :] = an_ref[...]


def _conv3(p, cw, not_first, not_last, tm):
    rows = p.shape[0]
    up = pltpu.roll(p, 1, 0)[HALO:HALO + tm]
    dn = pltpu.roll(p, rows - 1, 0)[HALO:HALO + tm]
    mid = p[HALO:HALO + tm]
    up = jnp.where(not_first > 0.0, up, 0.0)
    dn = jnp.where(not_last > 0.0, dn, 0.0)
    return cw[0:1] * up + cw[1:2] * mid + cw[2:3] * dn


def _up_ffn_kernel(a_ref, ap_ref, an_ref, edge_ref, wg_ref, wu_ref, cg_ref, cu_ref, o_ref, ext_ref, *, tm):
    _fill_ext(ext_ref, a_ref, ap_ref, an_ref, tm)
    e = edge_ref[...]
    nf, nl = e[:, 0:1], e[:, 1:2]
    a = ext_ref[...]
    g = _conv3(jnp.dot(a, wg_ref[...], preferred_element_type=F32), cg_ref[...], nf, nl, tm)
    u = _conv3(jnp.dot(a, wu_ref[...], preferred_element_type=F32), cu_ref[...], nf, nl, tm)
    o_ref[...] = (g * jax.nn.sigmoid(g) * u).astype(o_ref.dtype)


def _up_conv_kernel(a_ref, ap_ref, an_ref, edge_ref, w_ref, cw_ref, o_ref, ext_ref, *, tm):
    _fill_ext(ext_ref, a_ref, ap_ref, an_ref, tm)
    e = edge_ref[...]
    p = jnp.dot(ext_ref[...], w_ref[...], preferred_element_type=F32)
    o_ref[...] = _conv3(p, cw_ref[...], e[:, 0:1], e[:, 1:2], tm).astype(o_ref.dtype)


def _up_plain_kernel(a_ref, w_ref, o_ref):
    o_ref[...] = jnp.dot(a_ref[...], w_ref[...], preferred_element_type=F32).astype(o_ref.dtype)


def _halo_specs(tm, d, t):
    nh = t // HALO
    return [pl.BlockSpec((tm, d), lambda i, j: (i, 0)),
            pl.BlockSpec((HALO, d), lambda i, j: (jnp.maximum(i * (tm // HALO) - 1, 0), 0)),
            pl.BlockSpec((HALO, d), lambda i, j: (jnp.minimum((i + 1) * (tm // HALO), nh - 1), 0)),
            pl.BlockSpec((tm, 2), lambda i, j: (i, 0))]


def _up_ffn_call(h, edge, wg, wu, cg, cu, layer):
    t, d = h.shape
    f = wg.shape[2]
    tm = _divisor(t, 1024, HALO)
    tn = min(512, f)
    est = (2 * tm * d * 2 + (tm + 2 * HALO) * d * 2 + 4 * d * tn * 2 + 2 * tm * tn * 2
           + 8 * (tm + 2 * HALO) * tn * 4)
    return pl.pallas_call(
        functools.partial(_up_ffn_kernel, tm=tm),
        grid=(t // tm, pl.cdiv(f, tn)),
        in_specs=_halo_specs(tm, d, t) + [
            pl.BlockSpec((None, d, tn), lambda i, j: (layer, 0, j)),
            pl.BlockSpec((None, d, tn), lambda i, j: (layer, 0, j)),
            pl.BlockSpec((None, 3, tn), lambda i, j: (layer, 0, j)),
            pl.BlockSpec((None, 3, tn), lambda i, j: (layer, 0, j))],
        out_specs=pl.BlockSpec((tm, tn), lambda i, j: (i, j)),
        out_shape=jax.ShapeDtypeStruct((t, f), BF16),
        scratch_shapes=[pltpu.VMEM((tm + 2 * HALO, d), BF16)],
        compiler_params=_params(("parallel", "arbitrary"), est),
        name="ffn_up",
    )(h, h, h, edge, wg, wu, cg, cu)


def _up_conv_call(h, edge, w, cw, layer):
    t, d = h.shape
    n = w.shape[2]
    tm = _divisor(t, 1024, HALO)
    tn = _divisor(n, 512, LANES)
    est = (2 * tm * d * 2 + (tm + 2 * HALO) * d * 2 + 2 * d * tn * 2 + 2 * tm * tn * 4
           + 5 * (tm + 2 * HALO) * tn * 4)
    return pl.pallas_call(
        functools.partial(_up_conv_kernel, tm=tm),
        grid=(t // tm, n // tn),
        in_specs=_halo_specs(tm, d, t) + [
            pl.BlockSpec((None, d, tn), lambda i, j: (layer, 0, j)),
            pl.BlockSpec((None, 3, tn), lambda i, j: (layer, 0, j))],
        out_specs=pl.BlockSpec((tm, tn), lambda i, j: (i, j)),
        out_shape=jax.ShapeDtypeStruct((t, n), F32),
        scratch_shapes=[pltpu.VMEM((tm + 2 * HALO, d), BF16)],
        compiler_params=_params(("parallel", "arbitrary"), est),
        name="hyena_in",
    )(h, h, h, edge, w, cw)


def _up_plain_call(h, w, layer):
    t, d = h.shape
    n = w.shape[2]
    tm = _divisor(t, 1024, HALO)
    tn = _divisor(n, 512, LANES)
    est = 2 * tm * d * 2 + 2 * d * tn * 2 + 2 * tm * tn * 4 + 2 * tm * tn * 4
    return pl.pallas_call(
        _up_plain_kernel,
        grid=(t // tm, n // tn),
        in_specs=[pl.BlockSpec((tm, d), lambda i, j: (i, 0)),
                  pl.BlockSpec((None, d, tn), lambda i, j: (layer, 0, j))],
        out_specs=pl.BlockSpec((tm, tn), lambda i, j: (i, j)),
        out_shape=jax.ShapeDtypeStruct((t, n), F32),
        compiler_params=_params(("parallel", "arbitrary"), est),
        name="attn_qkv",
    )(h, w)


def _down_ln_kernel(*refs, nj, tn, d, gate, alpha, nxt):
    if nxt is None:
        a_ref, w_ref, x_ref, mod_ref, lg_ref, lb_ref, xo_ref = refs
    else:
        a_ref, w_ref, x_ref, mod_ref, modn_ref, lg_ref, lb_ref, xo_ref, ho_ref = refs
    j = pl.program_id(1)
    col = pl.multiple_of(j * tn, tn)
    xo_ref[:, pl.ds(col, tn)] = jnp.dot(a_ref[...], w_ref[...], preferred_element_type=F32)

    @pl.when(j == nj - 1)
    def _():
        g = mod_ref[:, gate * d:(gate + 1) * d]
        tm = xo_ref.shape[0]
        ch = math.gcd(tm, LN_CHUNK_ROWS)

        def chunk(r, carry):
            rows = pl.ds(pl.multiple_of(r * ch, ch), ch)
            xn = _layer_norm(alpha * x_ref[rows, :] + g * xo_ref[rows, :], lg_ref[...], lb_ref[...])
            xo_ref[rows, :] = xn
            if nxt is not None:
                sh = modn_ref[:, nxt * d:(nxt + 1) * d]
                sc = modn_ref[:, (nxt + 1) * d:(nxt + 2) * d]
                ho_ref[rows, :] = (xn * (1.0 + sc) + sh).astype(BF16)
            return carry

        lax.fori_loop(0, tm // ch, chunk, 0)


def _down_ln_call(a, w, wlayer, x, mods4, layer, gate, nxt_layer, nxt, lg, lb, dims, alpha, name):
    t, kdim = a.shape
    d = w.shape[2]
    tp, s_len, ds = dims
    tm = _divisor(math.gcd(tp, ds), 512, HALO)
    tn = _divisor(d, 512, LANES)
    nj = d // tn
    kern = functools.partial(_down_ln_kernel, nj=nj, tn=tn, d=d, gate=gate, alpha=alpha, nxt=nxt)
    fixed = 2 * kdim * tn * 2 + tm * d * 4 + 2 * tm * tn * 4 + (4 << 20)
    per_tile = tm * kdim * 2 + tm * d * 4 + (tm * d * 2 if nxt is not None else 0)
    once = pl.Buffered(1)
    rest = pl.Buffered(2) if fixed + 2 * per_tile <= V7X_VMEM_BUDGET else once
    mod_spec = lambda l: pl.BlockSpec((None, None, 1, 6 * d),
                                      lambda i, k: (l, _mod_row(i * tm, tp, ds), 0, 0))
    in_specs = [pl.BlockSpec((tm, kdim), lambda i, k: (i, 0), pipeline_mode=rest),
                pl.BlockSpec((None, kdim, tn), lambda i, k: (wlayer, 0, k)),
                pl.BlockSpec((tm, d), lambda i, k: (i, 0), pipeline_mode=once),
                mod_spec(layer)]
    args = [a, w, x, mods4]
    if nxt is not None:
        in_specs.append(mod_spec(nxt_layer))
        args.append(mods4)
    in_specs += [pl.BlockSpec((1, d), lambda i, k: (0, 0)), pl.BlockSpec((1, d), lambda i, k: (0, 0))]
    args += [lg.reshape(1, d), lb.reshape(1, d)]
    out_specs = [pl.BlockSpec((tm, d), lambda i, k: (i, 0), pipeline_mode=rest)]
    out_shape = [jax.ShapeDtypeStruct((t, d), F32)]
    if nxt is not None:
        out_specs.append(pl.BlockSpec((tm, d), lambda i, k: (i, 0), pipeline_mode=rest))
        out_shape.append(jax.ShapeDtypeStruct((t, d), BF16))
    est = fixed + (2 if rest is not once else 1) * per_tile
    res = pl.pallas_call(
        kern,
        grid=(t // tm, nj),
        in_specs=in_specs,
        out_specs=out_specs,
        out_shape=out_shape,
        compiler_params=_params(("parallel", "arbitrary"), est),
        name=name,
    )(*args)
    return (res[0], res[1]) if nxt is not None else (res[0], None)


def _dot_f32(a, b):
    return lax.dot_general(a, b, (((1,), (0,)), ((), ())), precision=lax.Precision.HIGHEST,
                           preferred_element_type=F32)


def _filter_kernel(z_ref, tu_ref, w1_ref, b1_ref, w2_ref, b2_ref, w3_ref, b3_ref, fr_ref, dec_ref,
                   hk_ref, as_ref, *, tl, n_col_tiles):
    jc = pl.program_id(0)
    i = pl.program_id(1)
    h = jnp.sin(fr_ref[0:1, :] * (_dot_f32(z_ref[...], w1_ref[...]) + b1_ref[...]))
    h = jnp.sin(fr_ref[1:2, :] * (_dot_f32(h, w2_ref[...]) + b2_ref[...]))
    h = _dot_f32(h, w3_ref[...]) + b3_ref[...]
    h = h * jnp.exp(-tu_ref[...] * jnp.abs(dec_ref[...]))
    row = i * tl + lax.broadcasted_iota(jnp.int32, (tl, 1), 0)
    drop = jnp.logical_and(row == 0, jc >= n_col_tiles // 2)
    h = jnp.where(drop, 0.0, h)
    hk_ref[...] = h.astype(BF16)
    part = jnp.sum(jnp.abs(h), axis=0, keepdims=True)

    @pl.when(i == 0)
    def _():
        as_ref[...] = part

    @pl.when(i > 0)
    def _():
        as_ref[...] += part


def _filter_call(length, w1, b1, w2, b2, w3, b3, freq, decay):
    emb, width = w1.shape
    n = w3.shape[1]
    t = jnp.arange(length, dtype=F32)
    t_unit = t / max(length - 1, 1)
    n_bands = (emb - 1) // 2
    bands = jnp.linspace(1e-4, n_bands - 1, n_bands, dtype=F32)
    ang = (2.0 * math.pi / length) * t[:, None] * bands[None, :]
    z = jnp.concatenate([t_unit[:, None], jnp.cos(ang), -jnp.sin(ang)], -1)
    z = jnp.pad(z, ((0, 0), (0, LANES - emb)))
    w1p = jnp.pad(w1, ((0, LANES - emb), (0, 0)))
    tl = _divisor(length, 256, 8)
    tn = _divisor(n // 2, 2048, LANES)
    nct = n // tn
    est = 2 * tl * tn * 2 + 6 * tl * tn * 4 + 4 * width * tn * 4
    return pl.pallas_call(
        functools.partial(_filter_kernel, tl=tl, n_col_tiles=nct),
        grid=(nct, length // tl),
        in_specs=[pl.BlockSpec((tl, LANES), lambda j, i: (i, 0)),
                  pl.BlockSpec((tl, 1), lambda j, i: (i, 0)),
                  pl.BlockSpec((LANES, width), lambda j, i: (0, 0)),
                  pl.BlockSpec((1, width), lambda j, i: (0, 0)),
                  pl.BlockSpec((width, width), lambda j, i: (0, 0)),
                  pl.BlockSpec((1, width), lambda j, i: (0, 0)),
                  pl.BlockSpec((width, tn), lambda j, i: (0, j)),
                  pl.BlockSpec((1, tn), lambda j, i: (0, j)),
                  pl.BlockSpec((2, width), lambda j, i: (0, 0)),
                  pl.BlockSpec((1, tn), lambda j, i: (0, j))],
        out_specs=[pl.BlockSpec((tl, tn), lambda j, i: (i, j)),
                   pl.BlockSpec((1, tn), lambda j, i: (0, j))],
        out_shape=[jax.ShapeDtypeStruct((length, n), BF16), jax.ShapeDtypeStruct((1, n), F32)],
        compiler_params=_params(("parallel", "arbitrary"), est),
        name="hyena_filter",
    )(z, t_unit[:, None], w1p, b1.reshape(1, width), w2, b2.reshape(1, width), w3, b3.reshape(1, n),
      freq, decay.reshape(1, n))


def _dft_tables(length, tq):
    n2 = 2 * length
    nq = length // tq
    n = jnp.arange(length, dtype=jnp.int32)
    pa = (jnp.arange(nq, dtype=jnp.int32)[:, None] * tq * n[None, :]) % n2
    pb = (jnp.arange(tq, dtype=jnp.int32)[:, None] * n[None, :]) % n2
    a = pa.astype(F32) * (2.0 * math.pi / n2)
    b = pb.astype(F32) * (2.0 * math.pi / n2)
    ca, sa = jnp.cos(a)[:, None, :], jnp.sin(a)[:, None, :]
    cb, sb = jnp.cos(b)[None, :, :], jnp.sin(b)[None, :, :]
    cos_rows = ca * cb - sa * sb
    sin_rows = -(sa * cb + ca * sb)
    nyq = jnp.where(n % 2 == 0, 1.0, -1.0)[None, None, :]
    first = jnp.logical_and(lax.broadcasted_iota(jnp.int32, (nq, tq, 1), 0) == 0,
                            lax.broadcasted_iota(jnp.int32, (nq, tq, 1), 1) == 0)
    sin_rows = jnp.where(first, nyq, sin_rows)
    wf = jnp.concatenate([cos_rows, sin_rows], axis=1).reshape(2 * length, length).astype(BF16)
    return wf, wf.T


def _kf_kernel(wf_ref, hf_ref, hb_ref, af_ref, ab_ref, kre_ref, kim_ref, *, tq, n2):
    q = pl.program_id(1)
    ff = jnp.dot(wf_ref[...], hf_ref[...], preferred_element_type=F32)
    fb = jnp.dot(wf_ref[...], hb_ref[...], preferred_element_type=F32)
    inv = 1.0 / (af_ref[...] + ab_ref[...] + 1e-6)
    f0 = (q * tq + lax.broadcasted_iota(jnp.int32, (tq, 1), 0)) == 0
    w = jnp.where(f0, 1.0 / n2, 2.0 / n2) * inv
    kre_ref[...] = (ff[:tq] + fb[:tq]) * w
    kim_ref[...] = jnp.where(f0, ff[tq:] + fb[tq:], ff[tq:] - fb[tq:]) * w


def _kf_call(wf, hk, asum, tq):
    length = hk.shape[0]
    n = hk.shape[1] // 2
    dt = _divisor(n, 512, LANES)
    nq = length // tq
    nb = n // dt
    est = 2 * 2 * tq * length * 2 + 4 * length * dt * 2 + 4 * tq * dt * 4 + 6 * 2 * tq * dt * 4
    return pl.pallas_call(
        functools.partial(_kf_kernel, tq=tq, n2=2 * length),
        grid=(nb, nq),
        in_specs=[pl.BlockSpec((2 * tq, length), lambda j, q: (q, 0)),
                  pl.BlockSpec((length, dt), lambda j, q: (0, j)),
                  pl.BlockSpec((length, dt), lambda j, q: (0, nb + j)),
                  pl.BlockSpec((1, dt), lambda j, q: (0, j)),
                  pl.BlockSpec((1, dt), lambda j, q: (0, nb + j))],
        out_specs=[pl.BlockSpec((tq, dt), lambda j, q: (q, j)),
                   pl.BlockSpec((tq, dt), lambda j, q: (q, j))],
        out_shape=[jax.ShapeDtypeStruct((length, n), F32), jax.ShapeDtypeStruct((length, n), F32)],
        compiler_params=_params(("parallel", "arbitrary"), est),
        name="hyena_filter_dft",
    )(wf, hk, hk, asum, asum)


def _dftconv_kernel(*refs, nb, nq, tq, dt, own_acc):
    wf_ref, wft_ref, kre_ref, kim_ref, skip_ref = refs[:5]
    u_refs = refs[5:5 + nb]
    g_refs = refs[5 + nb:5 + 2 * nb]
    o_ref = refs[5 + 2 * nb]
    ub_ref = refs[6 + 2 * nb]
    acc_ref = refs[7 + 2 * nb] if own_acc else o_ref
    q = pl.program_id(2)
    half = wft_ref.shape[0] // 2

    @pl.when(q == 0)
    def _():
        for b in range(nb):
            u = u_refs[b][...]
            ub_ref[:, b * dt:(b + 1) * dt] = u.astype(BF16)
            acc_ref[b] = u * skip_ref[...]

    uf = jnp.dot(wf_ref[...], ub_ref[...], preferred_element_type=F32)
    ure, uim = uf[:tq], uf[tq:]
    kre = jnp.concatenate([kre_ref[...]] * nb, axis=1)
    kim = jnp.concatenate([kim_ref[...]] * nb, axis=1)
    f0 = (q * tq + lax.broadcasted_iota(jnp.int32, (tq, 1), 0)) == 0
    yre = ure * kre - jnp.where(f0, 0.0, uim * kim)
    yim = jnp.where(f0, uim * kim, ure * kim + uim * kre)
    y = jnp.concatenate([yre, yim], axis=0).astype(BF16)
    for hh in range(2):
        rows = slice(hh * half, (hh + 1) * half)
        res = jnp.dot(wft_ref[rows, :], y, preferred_element_type=F32)
        for b in range(nb):
            acc_ref[b, rows, :] += res[:, b * dt:(b + 1) * dt]

    @pl.when(q == nq - 1)
    def _():
        for b in range(nb):
            o_ref[b] = (g_refs[b][...] * acc_ref[b]).astype(o_ref.dtype)


def _dftconv_call(tabs, kre, kim, korder, skip, u3, urow0, ucol, g3, grow0, gcol, n_seq, out_dtype, d):
    wf, wft, tq = tabs
    length = wft.shape[0]
    nq = length // tq
    long_seq = length > 1024
    dt = _divisor(d, 256 if long_seq else 512, LANES)
    nb = _divisor(n_seq, 2 if long_seq else 4, 1)
    nd = d // dt
    own_acc = out_dtype != F32
    once = pl.Buffered(1)
    blk = length * dt
    est = (4 * 2 * tq * length * 2 + 2 * nb * blk * 4 + nb * blk * (4 if own_acc else 0)
           + nb * blk * jnp.dtype(out_dtype).itemsize + nb * blk * 2 + nb * blk * 2 + 8 * tq * nb * dt * 4)
    kern = functools.partial(_dftconv_kernel, nb=nb, nq=nq, tq=tq, dt=dt, own_acc=own_acc)
    seq_specs = lambda row0, col: [
        pl.BlockSpec((None, length, dt), lambda s, j, q, b=b: (row0 + s * nb + b, 0, col * nd + j),
                     pipeline_mode=once) for b in range(nb)]
    scratch = [pltpu.VMEM((length, nb * dt), BF16)]
    if own_acc:
        scratch.append(pltpu.VMEM((nb, length, dt), F32))
    return pl.pallas_call(
        kern,
        grid=(n_seq // nb, nd, nq),
        in_specs=[pl.BlockSpec((2 * tq, length), lambda s, j, q: (q, 0)),
                  pl.BlockSpec((length, 2 * tq), lambda s, j, q: (0, q)),
                  pl.BlockSpec((tq, dt), lambda s, j, q: (q, korder * nd + j)),
                  pl.BlockSpec((tq, dt), lambda s, j, q: (q, korder * nd + j)),
                  pl.BlockSpec((1, dt), lambda s, j, q: (0, j))]
                 + seq_specs(urow0, ucol) + seq_specs(grow0, gcol),
        out_specs=pl.BlockSpec((nb, length, dt), lambda s, j, q: (s, 0, j), pipeline_mode=once),
        out_shape=jax.ShapeDtypeStruct((n_seq, length, d), out_dtype),
        scratch_shapes=scratch,
        compiler_params=_params(("parallel", "parallel", "arbitrary"), est),
        name="hyena_conv",
    )(wf, wft, kre, kim, skip, *([u3] * nb), *([g3] * nb))


def _rope(x, cos, sin, dh):
    outs = []
    for c in range(2):
        xc = x[:, c * dh:(c + 1) * dh]
        lane = lax.broadcasted_iota(jnp.int32, xc.shape, 1)
        first_half = (lane & 32) == 0
        sw = jnp.where(first_half, pltpu.roll(xc, dh - 32, 1), pltpu.roll(xc, 32, 1))
        outs.append(xc * cos + sw * sin)
    return jnp.concatenate(outs, axis=1)


def _rope_kv_kernel(k_ref, v_ref, cos_ref, sin_ref, ko_ref, vo_ref, *, dh):
    ko_ref[...] = _rope(k_ref[...], cos_ref[...], sin_ref[...], dh).astype(BF16)
    vo_ref[...] = v_ref[...].astype(BF16)


def _rope_kv_call(qkv, cos, sin, tp, d, dh):
    t = qkv.shape[0]
    ts = t - tp
    length = cos.shape[0]
    tr = _divisor(math.gcd(length, tp), 512, HALO)
    nh = d // (2 * dh)
    per_seq = length // tr
    est = 8 * tr * 2 * dh * 4 + 4 * tr * dh * 4
    return pl.pallas_call(
        functools.partial(_rope_kv_kernel, dh=dh),
        grid=(ts // tr, nh),
        in_specs=[pl.BlockSpec((tr, 2 * dh), lambda i, h: (tp // tr + i, nh + h)),
                  pl.BlockSpec((tr, 2 * dh), lambda i, h: (tp // tr + i, 2 * nh + h)),
                  pl.BlockSpec((tr, dh), lambda i, h: (i % per_seq, 0)),
                  pl.BlockSpec((tr, dh), lambda i, h: (i % per_seq, 0))],
        out_specs=[pl.BlockSpec((tr, 2 * dh), lambda i, h: (i, h)),
                   pl.BlockSpec((tr, 2 * dh), lambda i, h: (i, h))],
        out_shape=[jax.ShapeDtypeStruct((ts, d), BF16), jax.ShapeDtypeStruct((ts, d), BF16)],
        compiler_params=_params(("parallel", "parallel"), est),
        name="attn_rope_kv",
    )(qkv, qkv, cos, sin)


def _attn_kernel(*refs, latent, dh, lam_init):
    if latent:
        lam_ref, sub_ref, q_ref, k_ref, v_ref, ck_ref, cv_ref, cos_ref, sin_ref, o_ref = refs
    else:
        lam_ref, sub_ref, q_ref, k_ref, v_ref, o_ref = refs
    scale = dh ** -0.5
    lp = lam_ref[...]
    lam = (jnp.exp(jnp.sum(lp[0:1] * lp[1:2], axis=1, keepdims=True))
           - jnp.exp(jnp.sum(lp[2:3] * lp[3:4], axis=1, keepdims=True)) + lam_init)
    q = q_ref[...]
    if latent:
        q = _rope(q, cos_ref[...], sin_ref[...], dh)
    qb = q.astype(BF16)
    segs = [(k_ref, v_ref)] + ([(ck_ref, cv_ref)] if latent else [])
    chunks = []
    for kr, vr in segs:
        n = kr.shape[0]
        ck = math.gcd(n, KV_CHUNK)
        chunks += [(kr, vr, i * ck, ck) for i in range(n // ck)]
    nt = (((1,), (1,)), ((), ()))
    c2 = scale * math.log2(math.e)
    m = [None, None]
    l = [None, None]
    acc = [None, None]
    for kr, vr, r0, ck in chunks:
        vv = vr[r0:r0 + ck, :].astype(BF16)
        for c in range(2):
            kk = kr[r0:r0 + ck, c * dh:(c + 1) * dh].astype(BF16)
            s = lax.dot_general(qb[:, c * dh:(c + 1) * dh], kk, nt, preferred_element_type=F32)
            mc = jnp.max(s, axis=1, keepdims=True)
            m_new = mc if m[c] is None else jnp.maximum(m[c], mc)
            e = jnp.exp2((s - m_new) * c2)
            es = jnp.sum(e, axis=1, keepdims=True)
            pv = jnp.dot(e.astype(BF16), vv, preferred_element_type=F32)
            if m[c] is None:
                l[c], acc[c] = es, pv
            else:
                shrink = jnp.exp2((m[c] - m_new) * c2)
                l[c] = shrink * l[c] + es
                acc[c] = shrink * acc[c] + pv
            m[c] = m_new
    o = acc[0] * (1.0 / l[0]) - acc[1] * (lam / l[1])
    o = o * lax.rsqrt(jnp.mean(o * o, axis=1, keepdims=True) + SUBLN_EPSILON) * sub_ref[...] * (1.0 - lam_init)
    o_ref[...] = o.astype(o_ref.dtype)


def _attn_ctx_call(qkv, lam_p, subln, n_seq, length, d, dh, lam_init):
    nh = d // (2 * dh)
    tq = _divisor(length, 256, HALO)
    per_seq = length // tq
    est = 2 * (tq + 2 * length) * 2 * dh * 4 + 10 * tq * length * 4 + 4 * length * 2 * dh * 2
    return pl.pallas_call(
        functools.partial(_attn_kernel, latent=False, dh=dh, lam_init=lam_init),
        grid=(n_seq, nh, per_seq),
        in_specs=[pl.BlockSpec((4, dh), lambda b, h, i: (0, 0)),
                  pl.BlockSpec((1, 2 * dh), lambda b, h, i: (0, 0)),
                  pl.BlockSpec((tq, 2 * dh), lambda b, h, i: (b * per_seq + i, h)),
                  pl.BlockSpec((length, 2 * dh), lambda b, h, i: (b, nh + h)),
                  pl.BlockSpec((length, 2 * dh), lambda b, h, i: (b, 2 * nh + h))],
        out_specs=pl.BlockSpec((tq, 2 * dh), lambda b, h, i: (b * per_seq + i, h)),
        out_shape=jax.ShapeDtypeStruct((n_seq * length, d), BF16),
        compiler_params=_params(("parallel", "parallel", "arbitrary"), est),
        name="attn_context",
    )(lam_p, subln.reshape(1, 2 * dh), qkv, qkv, qkv)


def _attn_lat_call(qkv, kr, vb, ck, cv, cos, sin, lam_p, subln, tp, n_seq, length, d, dh, lam_init):
    nh = d // (2 * dh)
    lc = ck.shape[1]
    tq = _divisor(math.gcd(length, tp), 256, HALO)
    per_seq = length // tq
    est = (4 * (length + lc) * 2 * dh * 2 + 4 * lc * 2 * dh * 4 + 4 * tq * 2 * dh * 4
           + 5 * tq * (length + lc) * 4)
    return pl.pallas_call(
        functools.partial(_attn_kernel, latent=True, dh=dh, lam_init=lam_init),
        grid=(n_seq, nh, per_seq),
        in_specs=[pl.BlockSpec((4, dh), lambda b, h, i: (0, 0)),
                  pl.BlockSpec((1, 2 * dh), lambda b, h, i: (0, 0)),
                  pl.BlockSpec((tq, 2 * dh), lambda b, h, i: (tp // tq + b * per_seq + i, h)),
                  pl.BlockSpec((length, 2 * dh), lambda b, h, i: (b, h)),
                  pl.BlockSpec((length, 2 * dh), lambda b, h, i: (b, h)),
                  pl.BlockSpec((None, lc, 2 * dh), lambda b, h, i: (b, 0, h)),
                  pl.BlockSpec((None, lc, 2 * dh), lambda b, h, i: (b, 0, h)),
                  pl.BlockSpec((tq, dh), lambda b, h, i: (i, 0)),
                  pl.BlockSpec((tq, dh), lambda b, h, i: (i, 0))],
        out_specs=pl.BlockSpec((tq, 2 * dh), lambda b, h, i: (b * per_seq + i, h)),
        out_shape=jax.ShapeDtypeStruct((n_seq * length, d), BF16),
        compiler_params=_params(("parallel", "parallel", "arbitrary"), est),
        name="attn_latent",
    )(lam_p, subln.reshape(1, 2 * dh), qkv, kr, vb, ck, cv, cos, sin)


def _rope_tables(length, dh):
    half = dh // 2
    pos = jnp.arange(length)
    pos_r = (pos // GRID_COLS).astype(F32)
    pos_c = (pos % GRID_COLS).astype(F32)
    inv = jnp.power(ROPE_THETA, -jnp.arange(0, half, 2, dtype=F32) / half)
    ang_r = pos_r[:, None] * inv[None]
    ang_c = pos_c[:, None] * inv[None]
    cos = jnp.concatenate([jnp.cos(ang_r)] * 2 + [jnp.cos(ang_c)] * 2, axis=1)
    sin = jnp.concatenate([-jnp.sin(ang_r), jnp.sin(ang_r), -jnp.sin(ang_c), jnp.sin(ang_c)], axis=1)
    return cos, sin


def kernel(x_prompt, x_sample, cache_k, cache_v, c, c_ctx, mod_w, mod_b, ln1_g, ln1_b, ln2_g, ln2_b, pool_w, pool_scale, hyena_w_in, hyena_conv, hyena_ffn_w1, hyena_ffn_b1, hyena_ffn_w2, hyena_ffn_b2, hyena_ffn_w3, hyena_ffn_b3, hyena_freq, hyena_decay, hyena_skip, hyena_w_out, attn_w_qkv, attn_lambda, attn_subln, attn_w_out, ffn_w_up, ffn_conv, ffn_w_down):
    n_ctx, s_len, d = x_prompt.shape
    n_lat, ds, _ = x_sample.shape
    depth = mod_w.shape[0]
    tp, ts = n_ctx * s_len, n_lat * ds
    t = tp + ts
    dims = (tp, s_len, ds)
    alpha = (2 * depth) ** 0.25
    nh = cache_k.shape[3]
    dh = cache_k.shape[4] // 2
    lc = cache_k.shape[2]
    assert tp % ds == 0
    assert n_lat + 1 <= 8

    cond8 = jnp.zeros((8, d), F32).at[0].set(c_ctx).at[1:1 + n_lat].set(c)
    mods = _mods_call(cond8, mod_w, mod_b)
    mods4 = mods.reshape(depth, 8, 1, 6 * d)

    x = jnp.concatenate([x_prompt.reshape(tp, d), x_sample.reshape(ts, d)], axis=0)

    pos = jnp.concatenate([jnp.arange(tp) % s_len, jnp.arange(ts) % ds])
    last = jnp.concatenate([jnp.full((tp,), s_len - 1), jnp.full((ts,), ds - 1)])
    edge = jnp.stack([(pos != 0), (pos != last)], axis=1).astype(F32)

    w_down16 = ffn_w_down.astype(BF16)
    f = ffn_w_down.shape[1]
    w_gate16 = ffn_w_up[:, :, :f].astype(BF16)
    w_up16 = ffn_w_up[:, :, f:].astype(BF16)
    conv_gate, conv_up = ffn_conv[:, :, :f], ffn_conv[:, :, f:]
    hyena_in16 = hyena_w_in.astype(BF16)
    qkv16 = attn_w_qkv.astype(BF16)
    hyena_out16 = hyena_w_out.astype(BF16)
    attn_out16 = attn_w_out.astype(BF16)

    new_k = new_v = None
    h_in = None
    for i in range(depth):
        kind, j = i % N_MIXER_KINDS, i // N_MIXER_KINDS
        if kind == 0:
            x, h2 = _pool_call(x, mods4, i, pool_w[j].astype(BF16), pool_scale[j], ln1_g[i], ln1_b[i], dims, alpha)
        elif kind == 1:
            p = _up_conv_call(h_in, edge, hyena_in16, hyena_conv, j)
            z2s = []
            for length, n_seq, row0 in ((s_len, n_ctx, 0), (ds, n_lat, tp // ds)):
                hk, asum = _filter_call(length, hyena_ffn_w1[j], hyena_ffn_b1[j], hyena_ffn_w2[j], hyena_ffn_b2[j],
                                        hyena_ffn_w3[j], hyena_ffn_b3[j], hyena_freq[j], hyena_decay[j])
                tq = _divisor(length, 256, LANES)
                wf, wft = _dft_tables(length, tq)
                kre, kim = _kf_call(wf, hk, asum, tq)
                tabs = (wf, wft, tq)
                p3 = p.reshape(t // length, length, 3 * d)
                z1 = _dftconv_call(tabs, kre, kim, 0, hyena_skip[j, 0:1], p3, row0, 0, p3, row0, 1, n_seq, F32, d)
                z2 = _dftconv_call(tabs, kre, kim, 1, hyena_skip[j, 1:2], z1, 0, 0, p3, row0, 2, n_seq, BF16, d)
                z2s.append(z2.reshape(n_seq * length, d))
            z2 = jnp.concatenate(z2s, axis=0)
            x, h2 = _down_ln_call(z2, hyena_out16, j, x, mods4, i, 2, i, 3, ln1_g[i], ln1_b[i],
                                  dims, alpha, "hyena_out")
        else:
            lam_init = 0.8 - 0.6 * math.exp(-0.3 * i)
            qkv = _up_plain_call(h_in, qkv16, j)
            new_k = qkv[:tp, d:2 * d].reshape(n_ctx, s_len, nh, 2 * dh)
            new_v = qkv[:tp, 2 * d:3 * d].reshape(n_ctx, s_len, nh, 2 * dh)
            o_ctx = _attn_ctx_call(qkv, attn_lambda[j], attn_subln[j], n_ctx, s_len, d, dh, lam_init)
            cos, sin = _rope_tables(ds, dh)
            kr, vb = _rope_kv_call(qkv, cos, sin, tp, d, dh)
            o_lat = _attn_lat_call(qkv, kr, vb, cache_k[:, j].reshape(n_lat, lc, d), cache_v[:, j].reshape(n_lat, lc, d),
                                   cos, sin, attn_lambda[j], attn_subln[j], tp, n_lat, ds, d, dh, lam_init)
            o = jnp.concatenate([o_ctx, o_lat], axis=0)
            x, h2 = _down_ln_call(o, attn_out16, j, x, mods4, i, 2, i, 3, ln1_g[i], ln1_b[i],
                                  dims, alpha, "attn_out")
        a = _up_ffn_call(h2, edge, w_gate16, w_up16, conv_gate, conv_up, i)
        last_layer = i == depth - 1
        x, h_in = _down_ln_call(a, w_down16, i, x, mods4, i, 5, i + 1, None if last_layer else 0,
                                ln2_g[i], ln2_b[i], dims, alpha, "ffn_down")

    y_prompt = x[:tp].reshape(n_ctx, s_len, d)
    y_sample = x[tp:].reshape(n_lat, ds, d)
    return (y_prompt, y_sample, new_k[:, None], new_v[:, None])
```

```python
import functools
import math

import jax
import jax.numpy as jnp
from jax import lax
from jax.experimental import pallas as pl
from jax.experimental.pallas import tpu as pltpu

F32 = jnp.float32
BF16 = jnp.bfloat16

GRID_COLS = 64
POOL_WINDOW_SIZES = (2, 4, 8, 16)
ROPE_THETA = 10000.0
LN_EPSILON = 1e-6
SUBLN_EPSILON = 1e-5
N_MIXER_KINDS = 3

V7X_VMEM_BYTES = 64 * 1024 * 1024
V7X_VMEM_BUDGET = V7X_VMEM_BYTES - 6 * 1024 * 1024
SUBLANES_F32 = 8
SUBLANES_BF16 = 16
LANES = 128
KV_CHUNK = 1024
LN_CHUNK_ROWS = 64


def _params(sem, est_bytes):
    limit = int(min(V7X_VMEM_BUDGET, max(32 * 1024 * 1024, est_bytes * 5 // 4)))
    return pltpu.CompilerParams(dimension_semantics=sem, vmem_limit_bytes=limit)


def _divisor(n, pref, mult):
    best = None
    d = mult
    while d <= min(n, pref):
        if n % d == 0:
            best = d
        d += mult
    assert best is not None, (n, pref, mult)
    return best


def _layer_norm(r, g, b):
    mu = jnp.mean(r, -1, keepdims=True)
    d = r - mu
    var = jnp.mean(d * d, -1, keepdims=True)
    return d * lax.rsqrt(var + LN_EPSILON) * g + b


def _mod_row(t0, tp, ds):
    return jnp.where(t0 < tp, 0, 1 + (t0 - tp) // ds)


def _mods_kernel(c_ref, w_ref, b_ref, o_ref):
    k = pl.program_id(2)

    @pl.when(k == 0)
    def _():
        o_ref[0] = jnp.broadcast_to(b_ref[0], o_ref.shape[1:])

    c = c_ref[...]
    s = (c * jax.nn.sigmoid(c)).astype(BF16)
    o_ref[0] += jnp.dot(s, w_ref[0].astype(BF16), preferred_element_type=F32)


def _mods_call(cond8, mod_w, mod_b):
    depth, d, n = mod_w.shape
    tk = _divisor(d, 2048, LANES)
    tn = _divisor(n, 1024, LANES)
    est = 2 * tk * tn * 4 + 4 * 8 * tn * 4 + 2 * 8 * tk * 4 + tk * tn * 2
    return pl.pallas_call(
        _mods_kernel,
        grid=(depth, n // tn, d // tk),
        in_specs=[pl.BlockSpec((8, tk), lambda l, j, k: (0, k)),
                  pl.BlockSpec((1, tk, tn), lambda l, j, k: (l, k, j)),
                  pl.BlockSpec((1, 1, tn), lambda l, j, k: (l, 0, j))],
        out_specs=pl.BlockSpec((1, 8, tn), lambda l, j, k: (l, 0, j)),
        out_shape=jax.ShapeDtypeStruct((depth, 8, n), F32),
        compiler_params=_params(("parallel", "parallel", "arbitrary"), est),
        name="mods",
    )(cond8, mod_w, mod_b.reshape(depth, 1, n))


def _pool_kernel(xm_ref, xp_ref, xn_ref, mod_ref, pw_ref, ps_ref, lg_ref, lb_ref, xo_ref, ho_ref, e_ref,
                 *, tm, tp, s_len, ds, d, alpha):
    t0 = pl.program_id(0) * tm
    is_ctx = t0 < tp
    seq_len = jnp.where(is_ctx, s_len, ds)
    pos0 = jnp.where(is_ctx, t0 % s_len, (t0 - tp) % ds)
    mods = mod_ref[...]
    sh1, sc1, g1 = mods[:, 0:d], mods[:, d:2 * d], mods[:, 2 * d:3 * d]
    sh2, sc2 = mods[:, 3 * d:4 * d], mods[:, 4 * d:5 * d]
    x = xm_ref[...]
    h = x * (1.0 + sc1) + sh1
    has_prev = pos0 > 0
    has_next = pos0 + tm < seq_len
    e_ref[0:8, :] = jnp.where(has_prev, xp_ref[...] * (1.0 + sc1) + sh1, 0.0)
    e_ref[8:8 + tm, :] = h
    e_ref[8 + tm:16 + tm, :] = jnp.where(has_next, xn_ref[...] * (1.0 + sc1) + sh1, 0.0)
    t = pos0 + lax.broadcasted_iota(jnp.int32, (tm, 1), 0)
    cg = d // len(POOL_WINDOW_SIZES)
    ys = []
    for g, w in enumerate(POOL_WINDOW_SIZES):
        c0 = g * cg
        acc = None
        for s in range(-(w // 2), w - w // 2):
            v = e_ref[8 + s:8 + s + tm, c0:c0 + cg]
            acc = v if acc is None else acc + v
        lo = jnp.maximum(t - w // 2, 0)
        hi = jnp.minimum(t + (w - w // 2), seq_len)
        inv = 1.0 / (hi - lo).astype(F32)
        p = acc * inv - h[:, c0:c0 + cg]
        ys.append(jnp.dot(p.astype(BF16), pw_ref[g], preferred_element_type=F32))
    y = jnp.concatenate(ys, axis=1) * ps_ref[...]
    xn = _layer_norm(alpha * x + g1 * y, lg_ref[...], lb_ref[...])
    xo_ref[...] = xn
    ho_ref[...] = (xn * (1.0 + sc2) + sh2).astype(BF16)


def _pool_call(x, mods4, layer, pw, ps, lg, lb, dims, alpha):
    t, d = x.shape
    tp, s_len, ds = dims
    tm = s_len
    assert ds % tm == 0 and tm % SUBLANES_BF16 == 0
    n8 = t // 8
    kern = functools.partial(_pool_kernel, tm=tm, tp=tp, s_len=s_len, ds=ds, d=d, alpha=alpha)
    g, cg, _ = pw.shape
    est = (2 * tm * d * 4 * 2 + 2 * tm * d * 2 + (tm + 16) * d * 4 + 2 * g * cg * cg * 2
           + 6 * tm * d * 4)
    return pl.pallas_call(
        kern,
        grid=(t // tm,),
        in_specs=[pl.BlockSpec((tm, d), lambda i: (i, 0)),
                  pl.BlockSpec((8, d), lambda i: (jnp.maximum(i * (tm // 8) - 1, 0), 0)),
                  pl.BlockSpec((8, d), lambda i: (jnp.minimum((i + 1) * (tm // 8), n8 - 1), 0)),
                  pl.BlockSpec((None, None, 1, 6 * d), lambda i: (layer, _mod_row(i * tm, tp, ds), 0, 0)),
                  pl.BlockSpec((g, cg, cg), lambda i: (0, 0, 0)),
                  pl.BlockSpec((1, d), lambda i: (0, 0)),
                  pl.BlockSpec((1, d), lambda i: (0, 0)),
                  pl.BlockSpec((1, d), lambda i: (0, 0))],
        out_specs=[pl.BlockSpec((tm, d), lambda i: (i, 0)),
                   pl.BlockSpec((tm, d), lambda i: (i, 0))],
        out_shape=[jax.ShapeDtypeStruct((t, d), F32), jax.ShapeDtypeStruct((t, d), BF16)],
        scratch_shapes=[pltpu.VMEM((tm + 16, d), F32)],
        compiler_params=_params(("parallel",), est),
        name="pool_mixer",
    )(x, x, x, mods4, pw, ps.reshape(1, d), lg.reshape(1, d), lb.reshape(1, d))


HALO = SUBLANES_BF16


def _fill_ext(ext_ref, a_ref, ap_ref, an_ref, tm):
    @pl.when(pl.program_id(1) == 0)
    def _():
        ext_ref[0:HALO, :] = ap_ref[...]
        ext_ref[HALO:HALO + tm, :] = a_ref[...]
        ext_ref[HALO + tm:2 * HALO + tm, :] = an_ref[...]


def _conv3(p, cw, not_first, not_last, tm):
    rows = p.shape[0]
    up = pltpu.roll(p, 1, 0)[HALO:HALO + tm]
    dn = pltpu.roll(p, rows - 1, 0)[HALO:HALO + tm]
    mid = p[HALO:HALO + tm]
    up = jnp.where(not_first > 0.0, up, 0.0)
    dn = jnp.where(not_last > 0.0, dn, 0.0)
    return cw[0:1] * up + cw[1:2] * mid + cw[2:3] * dn


def _up_ffn_kernel(a_ref, ap_ref, an_ref, edge_ref, wg_ref, wu_ref, cg_ref, cu_ref, o_ref, ext_ref, *, tm):
    _fill_ext(ext_ref, a_ref, ap_ref, an_ref, tm)
    e = edge_ref[...]
    nf, nl = e[:, 0:1], e[:, 1:2]
    a = ext_ref[...]
    g = _conv3(jnp.dot(a, wg_ref[...], preferred_element_type=F32), cg_ref[...], nf, nl, tm)
    u = _conv3(jnp.dot(a, wu_ref[...], preferred_element_type=F32), cu_ref[...], nf, nl, tm)
    o_ref[...] = (g * jax.nn.sigmoid(g) * u).astype(o_ref.dtype)


def _up_conv_kernel(a_ref, ap_ref, an_ref, edge_ref, w_ref, cw_ref, o_ref, ext_ref, *, tm):
    _fill_ext(ext_ref, a_ref, ap_ref, an_ref, tm)
    e = edge_ref[...]
    p = jnp.dot(ext_ref[...], w_ref[...], preferred_element_type=F32)
    o_ref[...] = _conv3(p, cw_ref[...], e[:, 0:1], e[:, 1:2], tm).astype(o_ref.dtype)


def _up_plain_kernel(a_ref, w_ref, o_ref):
    o_ref[...] = jnp.dot(a_ref[...], w_ref[...], preferred_element_type=F32).astype(o_ref.dtype)


def _halo_specs(tm, d, t):
    nh = t // HALO
    return [pl.BlockSpec((tm, d), lambda i, j: (i, 0)),
            pl.BlockSpec((HALO, d), lambda i, j: (jnp.maximum(i * (tm // HALO) - 1, 0), 0)),
            pl.BlockSpec((HALO, d), lambda i, j: (jnp.minimum((i + 1) * (tm // HALO), nh - 1), 0)),
            pl.BlockSpec((tm, 2), lambda i, j: (i, 0))]


def _up_ffn_call(h, edge, wg, wu, cg, cu, layer):
    t, d = h.shape
    f = wg.shape[2]
    tm = _divisor(t, 1024, HALO)
    tn = min(512, f)
    est = (2 * tm * d * 2 + (tm + 2 * HALO) * d * 2 + 4 * d * tn * 2 + 2 * tm * tn * 2
           + 8 * (tm + 2 * HALO) * tn * 4)
    return pl.pallas_call(
        functools.partial(_up_ffn_kernel, tm=tm),
        grid=(t // tm, pl.cdiv(f, tn)),
        in_specs=_halo_specs(tm, d, t) + [
            pl.BlockSpec((None, d, tn), lambda i, j: (layer, 0, j)),
            pl.BlockSpec((None, d, tn), lambda i, j: (layer, 0, j)),
            pl.BlockSpec((None, 3, tn), lambda i, j: (layer, 0, j)),
            pl.BlockSpec((None, 3, tn), lambda i, j: (layer, 0, j))],
        out_specs=pl.BlockSpec((tm, tn), lambda i, j: (i, j)),
        out_shape=jax.ShapeDtypeStruct((t, f), BF16),
        scratch_shapes=[pltpu.VMEM((tm + 2 * HALO, d), BF16)],
        compiler_params=_params(("parallel", "arbitrary"), est),
        name="ffn_up",
    )(h, h, h, edge, wg, wu, cg, cu)


def _up_conv_call(h, edge, w, cw, layer):
    t, d = h.shape
    n = w.shape[2]
    tm = _divisor(t, 1024, HALO)
    tn = _divisor(n, 512, LANES)
    est = (2 * tm * d * 2 + (tm + 2 * HALO) * d * 2 + 2 * d * tn * 2 + 2 * tm * tn * 4
           + 5 * (tm + 2 * HALO) * tn * 4)
    return pl.pallas_call(
        functools.partial(_up_conv_kernel, tm=tm),
        grid=(t // tm, n // tn),
        in_specs=_halo_specs(tm, d, t) + [
            pl.BlockSpec((None, d, tn), lambda i, j: (layer, 0, j)),
            pl.BlockSpec((None, 3, tn), lambda i, j: (layer, 0, j))],
        out_specs=pl.BlockSpec((tm, tn), lambda i, j: (i, j)),
        out_shape=jax.ShapeDtypeStruct((t, n), F32),
        scratch_shapes=[pltpu.VMEM((tm + 2 * HALO, d), BF16)],
        compiler_params=_params(("parallel", "arbitrary"), est),
        name="hyena_in",
    )(h, h, h, edge, w, cw)


def _up_plain_call(h, w, layer):
    t, d = h.shape
    n = w.shape[2]
    tm = _divisor(t, 1024, HALO)
    tn = _divisor(n, 512, LANES)
    est = 2 * tm * d * 2 + 2 * d * tn * 2 + 2 * tm * tn * 4 + 2 * tm * tn * 4
    return pl.pallas_call(
        _up_plain_kernel,
        grid=(t // tm, n // tn),
        in_specs=[pl.BlockSpec((tm, d), lambda i, j: (i, 0)),
                  pl.BlockSpec((None, d, tn), lambda i, j: (layer, 0, j))],
        out_specs=pl.BlockSpec((tm, tn), lambda i, j: (i, j)),
        out_shape=jax.ShapeDtypeStruct((t, n), F32),
        compiler_params=_params(("parallel", "arbitrary"), est),
        name="attn_qkv",
    )(h, w)


def _down_ln_kernel(*refs, nj, tn, d, gate, alpha, nxt):
    if nxt is None:
        a_ref, w_ref, x_ref, mod_ref, lg_ref, lb_ref, xo_ref = refs
    else:
        a_ref, w_ref, x_ref, mod_ref, modn_ref, lg_ref, lb_ref, xo_ref, ho_ref = refs
    j = pl.program_id(1)
    col = pl.multiple_of(j * tn, tn)
    xo_ref[:, pl.ds(col, tn)] = jnp.dot(a_ref[...], w_ref[...], preferred_element_type=F32)

    @pl.when(j == nj - 1)
    def _():
        g = mod_ref[:, gate * d:(gate + 1) * d]
        tm = xo_ref.shape[0]
        ch = math.gcd(tm, LN_CHUNK_ROWS)

        def chunk(r, carry):
            rows = pl.ds(pl.multiple_of(r * ch, ch), ch)
            xn = _layer_norm(alpha * x_ref[rows, :] + g * xo_ref[rows, :], lg_ref[...], lb_ref[...])
            xo_ref[rows, :] = xn
            if nxt is not None:
                sh = modn_ref[:, nxt * d:(nxt + 1) * d]
                sc = modn_ref[:, (nxt + 1) * d:(nxt + 2) * d]
                ho_ref[rows, :] = (xn * (1.0 + sc) + sh).astype(BF16)
            return carry

        lax.fori_loop(0, tm // ch, chunk, 0)


def _down_ln_call(a, w, wlayer, x, mods4, layer, gate, nxt_layer, nxt, lg, lb, dims, alpha, name):
    t, kdim = a.shape
    d = w.shape[2]
    tp, s_len, ds = dims
    tm = _divisor(math.gcd(tp, ds), 512, HALO)
    tn = _divisor(d, 512, LANES)
    nj = d // tn
    kern = functools.partial(_down_ln_kernel, nj=nj, tn=tn, d=d, gate=gate, alpha=alpha, nxt=nxt)
    fixed = 2 * kdim * tn * 2 + tm * d * 4 + 2 * tm * tn * 4 + (4 << 20)
    per_tile = tm * kdim * 2 + tm * d * 4 + (tm * d * 2 if nxt is not None else 0)
    once = pl.Buffered(1)
    rest = pl.Buffered(2) if fixed + 2 * per_tile <= V7X_VMEM_BUDGET else once
    mod_spec = lambda l: pl.BlockSpec((None, None, 1, 6 * d),
                                      lambda i, k: (l, _mod_row(i * tm, tp, ds), 0, 0))
    in_specs = [pl.BlockSpec((tm, kdim), lambda i, k: (i, 0), pipeline_mode=rest),
                pl.BlockSpec((None, kdim, tn), lambda i, k: (wlayer, 0, k)),
                pl.BlockSpec((tm, d), lambda i, k: (i, 0), pipeline_mode=once),
                mod_spec(layer)]
    args = [a, w, x, mods4]
    if nxt is not None:
        in_specs.append(mod_spec(nxt_layer))
        args.append(mods4)
    in_specs += [pl.BlockSpec((1, d), lambda i, k: (0, 0)), pl.BlockSpec((1, d), lambda i, k: (0, 0))]
    args += [lg.reshape(1, d), lb.reshape(1, d)]
    out_specs = [pl.BlockSpec((tm, d), lambda i, k: (i, 0), pipeline_mode=rest)]
    out_shape = [jax.ShapeDtypeStruct((t, d), F32)]
    if nxt is not None:
        out_specs.append(pl.BlockSpec((tm, d), lambda i, k: (i, 0), pipeline_mode=rest))
        out_shape.append(jax.ShapeDtypeStruct((t, d), BF16))
    est = fixed + (2 if rest is not once else 1) * per_tile
    res = pl.pallas_call(
        kern,
        grid=(t // tm, nj),
        in_specs=in_specs,
        out_specs=out_specs,
        out_shape=out_shape,
        compiler_params=_params(("parallel", "arbitrary"), est),
        name=name,
    )(*args)
    return (res[0], res[1]) if nxt is not None else (res[0], None)


def _dot_f32(a, b):
    return lax.dot_general(a, b, (((1,), (0,)), ((), ())), precision=lax.Precision.HIGHEST,
                           preferred_element_type=F32)


def _filter_kernel(z_ref, tu_ref, w1_ref, b1_ref, w2_ref, b2_ref, w3_ref, b3_ref, fr_ref, dec_ref,
                   hk_ref, as_ref, *, tl, n_col_tiles):
    jc = pl.program_id(0)
    i = pl.program_id(1)
    h = jnp.sin(fr_ref[0:1, :] * (_dot_f32(z_ref[...], w1_ref[...]) + b1_ref[...]))
    h = jnp.sin(fr_ref[1:2, :] * (_dot_f32(h, w2_ref[...]) + b2_ref[...]))
    h = jnp.dot(h.astype(BF16), w3_ref[...].astype(BF16), preferred_element_type=F32) + b3_ref[...]
    h = h * jnp.exp(-tu_ref[...] * jnp.abs(dec_ref[...]))
    row = i * tl + lax.broadcasted_iota(jnp.int32, (tl, 1), 0)
    drop = jnp.logical_and(row == 0, jc >= n_col_tiles // 2)
    h = jnp.where(drop, 0.0, h)
    hk_ref[...] = h.astype(BF16)
    part = jnp.sum(jnp.abs(h), axis=0, keepdims=True)

    @pl.when(i == 0)
    def _():
        as_ref[...] = part

    @pl.when(i > 0)
    def _():
        as_ref[...] += part


def _filter_call(length, w1, b1, w2, b2, w3, b3, freq, decay):
    emb, width = w1.shape
    n = w3.shape[1]
    t = jnp.arange(length).reshape(length // 2, 2).T.reshape(length).astype(F32)
    t_unit = t / max(length - 1, 1)
    n_bands = (emb - 1) // 2
    bands = jnp.linspace(1e-4, n_bands - 1, n_bands, dtype=F32)
    ang = (2.0 * math.pi / length) * t[:, None] * bands[None, :]
    z = jnp.concatenate([t_unit[:, None], jnp.cos(ang), -jnp.sin(ang)], -1)
    z = jnp.pad(z, ((0, 0), (0, LANES - emb)))
    w1p = jnp.pad(w1, ((0, LANES - emb), (0, 0)))
    tl = _divisor(length, 256, 8)
    tn = _divisor(n // 2, 2048, LANES)
    nct = n // tn
    est = 2 * tl * tn * 2 + 6 * tl * tn * 4 + 4 * width * tn * 4
    return pl.pallas_call(
        functools.partial(_filter_kernel, tl=tl, n_col_tiles=nct),
        grid=(nct, length // tl),
        in_specs=[pl.BlockSpec((tl, LANES), lambda j, i: (i, 0)),
                  pl.BlockSpec((tl, 1), lambda j, i: (i, 0)),
                  pl.BlockSpec((LANES, width), lambda j, i: (0, 0)),
                  pl.BlockSpec((1, width), lambda j, i: (0, 0)),
                  pl.BlockSpec((width, width), lambda j, i: (0, 0)),
                  pl.BlockSpec((1, width), lambda j, i: (0, 0)),
                  pl.BlockSpec((width, tn), lambda j, i: (0, j)),
                  pl.BlockSpec((1, tn), lambda j, i: (0, j)),
                  pl.BlockSpec((2, width), lambda j, i: (0, 0)),
                  pl.BlockSpec((1, tn), lambda j, i: (0, j))],
        out_specs=[pl.BlockSpec((tl, tn), lambda j, i: (i, j)),
                   pl.BlockSpec((1, tn), lambda j, i: (0, j))],
        out_shape=[jax.ShapeDtypeStruct((length, n), BF16), jax.ShapeDtypeStruct((1, n), F32)],
        compiler_params=_params(("parallel", "arbitrary"), est),
        name="hyena_filter",
    )(z, t_unit[:, None], w1p, b1.reshape(1, width), w2, b2.reshape(1, width), w3, b3.reshape(1, n),
      freq, decay.reshape(1, n))


def _dft_tables(length, tq):
    n2 = 2 * length
    nq = length // tq
    n = jnp.arange(length, dtype=jnp.int32)
    pa = (jnp.arange(nq, dtype=jnp.int32)[:, None] * tq * n[None, :]) % n2
    pb = (jnp.arange(tq, dtype=jnp.int32)[:, None] * n[None, :]) % n2
    a = pa.astype(F32) * (2.0 * math.pi / n2)
    b = pb.astype(F32) * (2.0 * math.pi / n2)
    ca, sa = jnp.cos(a)[:, None, :], jnp.sin(a)[:, None, :]
    cb, sb = jnp.cos(b)[None, :, :], jnp.sin(b)[None, :, :]
    cos_rows = ca * cb - sa * sb
    sin_rows = -(sa * cb + ca * sb)
    nyq = jnp.where(n % 2 == 0, 1.0, -1.0)[None, None, :]
    first = jnp.logical_and(lax.broadcasted_iota(jnp.int32, (nq, tq, 1), 0) == 0,
                            lax.broadcasted_iota(jnp.int32, (nq, tq, 1), 1) == 0)
    sin_rows = jnp.where(first, nyq, sin_rows)
    wf = jnp.concatenate([cos_rows, sin_rows], axis=1).reshape(2 * length, length).astype(BF16)
    return wf, wf.T


def _fold_fwd(e, o, c, s, f0, tq):
    ere, eim, ore, oim = e[:tq], e[tq:], o[:tq], o[tq:]
    tre = ore * c + oim * s
    tim = oim * c - ore * s
    pre = ere + tre
    pim = jnp.where(f0, ere - ore, eim + tim)
    qre = jnp.where(f0, eim, ere - tre)
    qim = jnp.where(f0, -oim, tim - eim)
    return pre, pim, qre, qim


def _kf_kernel(wf_ref, fe_ref, fo_ref, be_ref, bo_ref, af_ref, ab_ref, tw_ref,
               kpre_ref, kpim_ref, kqre_ref, kqim_ref, *, tq, n2):
    q = pl.program_id(1)
    w = wf_ref[...]
    dot = lambda r: jnp.dot(w, r[...], preferred_element_type=F32)
    f0 = (q * tq + lax.broadcasted_iota(jnp.int32, (tq, 1), 0)) == 0
    c, s = tw_ref[:, 0:1], tw_ref[:, 1:2]
    fpre, fpim, fqre, fqim = _fold_fwd(dot(fe_ref), dot(fo_ref), c, s, f0, tq)
    bpre, bpim, bqre, bqim = _fold_fwd(dot(be_ref), dot(bo_ref), c, s, f0, tq)
    inv = 1.0 / (af_ref[...] + ab_ref[...] + 1e-6)
    wq = (2.0 / n2) * inv
    wp = jnp.where(f0, 1.0 / n2, 2.0 / n2) * inv
    kpre_ref[...] = (fpre + bpre) * wp
    kpim_ref[...] = jnp.where(f0, fpim + bpim, fpim - bpim) * wp
    kqre_ref[...] = (fqre + bqre) * wq
    kqim_ref[...] = (fqim - bqim) * wq


def _kf_call(wf, tw, hk, asum, tq):
    half = hk.shape[0] // 2
    n = hk.shape[1] // 2
    dt = _divisor(n, 512, LANES)
    nq = half // tq
    nb = n // dt
    est = 2 * 2 * tq * half * 2 + 8 * half * dt * 2 + 8 * tq * dt * 4 + 24 * tq * dt * 4
    plane = jax.ShapeDtypeStruct((half, n), F32)
    taps = lambda parity, col0: pl.BlockSpec((half, dt), lambda j, q: (parity, col0 + j))
    return pl.pallas_call(
        functools.partial(_kf_kernel, tq=tq, n2=4 * half),
        grid=(nb, nq),
        in_specs=[pl.BlockSpec((2 * tq, half), lambda j, q: (q, 0)),
                  taps(0, 0), taps(1, 0), taps(0, nb), taps(1, nb),
                  pl.BlockSpec((1, dt), lambda j, q: (0, j)),
                  pl.BlockSpec((1, dt), lambda j, q: (0, nb + j)),
                  pl.BlockSpec((tq, 2), lambda j, q: (q, 0))],
        out_specs=[pl.BlockSpec((tq, dt), lambda j, q: (q, j))] * 4,
        out_shape=[plane] * 4,
        compiler_params=_params(("parallel", "arbitrary"), est),
        name="hyena_filter_dft",
    )(wf, hk, hk, hk, hk, asum, asum, tw)


def _dftconv_kernel(*refs, nb, nq, tq, dt):
    wf_ref, wft_ref, tw_ref, kpre_ref, kpim_ref, kqre_ref, kqim_ref, skip_ref = refs[:8]
    seq = lambda k: refs[8 + k * nb:8 + (k + 1) * nb]
    ue_in, uo_in, ge_in, go_in = seq(0), seq(1), seq(2), seq(3)
    oe_ref, oo_ref = refs[8 + 4 * nb:10 + 4 * nb]
    ue_ref, uo_ref, ae_ref, ao_ref = refs[10 + 4 * nb:]
    q = pl.program_id(2)

    @pl.when(q == 0)
    def _():
        for b in range(nb):
            cols = slice(b * dt, (b + 1) * dt)
            ue = ue_in[b][...]
            uo = uo_in[b][...]
            ue_ref[:, cols] = ue.astype(BF16)
            uo_ref[:, cols] = uo.astype(BF16)
            ae_ref[:, cols] = ue * skip_ref[...]
            ao_ref[:, cols] = uo * skip_ref[...]

    w = wf_ref[...]
    f0 = (q * tq + lax.broadcasted_iota(jnp.int32, (tq, 1), 0)) == 0
    c, s = tw_ref[:, 0:1], tw_ref[:, 1:2]
    pre, pim, qre, qim = _fold_fwd(jnp.dot(w, ue_ref[...], preferred_element_type=F32),
                                   jnp.dot(w, uo_ref[...], preferred_element_type=F32), c, s, f0, tq)
    wide = lambda r: jnp.concatenate([r[...]] * nb, axis=1)
    kpre, kpim, kqre, kqim = wide(kpre_ref), wide(kpim_ref), wide(kqre_ref), wide(kqim_ref)
    ypre = pre * kpre - jnp.where(f0, 0.0, pim * kpim)
    ypim = jnp.where(f0, pim * kpim, pre * kpim + pim * kpre)
    yqre = qre * kqre - qim * kqim
    yqim = qre * kqim + qim * kqre
    dre, dim = ypre - yqre, ypim + yqim
    z0re = jnp.where(f0, ypre + ypim, ypre + yqre)
    z0im = jnp.where(f0, yqre, ypim - yqim)
    z1re = jnp.where(f0, ypre - ypim, dre * c - dim * s)
    z1im = jnp.where(f0, -yqim, dre * s + dim * c)
    wt = wft_ref[...]
    ae_ref[...] += jnp.dot(wt, jnp.concatenate([z0re, z0im], axis=0).astype(BF16), preferred_element_type=F32)
    ao_ref[...] += jnp.dot(wt, jnp.concatenate([z1re, z1im], axis=0).astype(BF16), preferred_element_type=F32)

    @pl.when(q == nq - 1)
    def _():
        for b in range(nb):
            cols = slice(b * dt, (b + 1) * dt)
            oe_ref[b] = ge_in[b][...] * ae_ref[:, cols]
            oo_ref[b] = go_in[b][...] * ao_ref[:, cols]


def _dftconv_call(tabs, planes, korder, skip, u_even, u_odd, g_even, g_odd, n_seq, d):
    wf, wft, tw, tq = tabs
    half = wft.shape[0]
    nq = half // tq
    long_seq = half > 512
    dt = _divisor(d, 256 if long_seq else 512, LANES)
    nb = _divisor(n_seq, 2 if long_seq else 4, 1)
    nd = d // dt
    once = pl.Buffered(1)
    blk = half * dt
    est = (4 * 2 * tq * half * 2 + 6 * nb * blk * 4 + 2 * nb * blk * 2 + 2 * nb * blk * 4
           + 2 * nb * blk * 4 + 40 * tq * nb * dt * 4)
    kern = functools.partial(_dftconv_kernel, nb=nb, nq=nq, tq=tq, dt=dt)
    sources = (u_even, u_odd, g_even, g_odd)
    seq_specs = lambda row0, col: [
        pl.BlockSpec((None, half, dt), lambda s, j, q, b=b: (row0 + s * nb + b, 0, col * nd + j),
                     pipeline_mode=once) for b in range(nb)]
    plane_spec = pl.BlockSpec((tq, dt), lambda s, j, q: (q, korder * nd + j))
    out_spec = pl.BlockSpec((nb, half, dt), lambda s, j, q: (s, 0, j), pipeline_mode=once)
    parity = jax.ShapeDtypeStruct((n_seq, half, d), F32)
    return pl.pallas_call(
        kern,
        grid=(n_seq // nb, nd, nq),
        in_specs=[pl.BlockSpec((2 * tq, half), lambda s, j, q: (q, 0)),
                  pl.BlockSpec((half, 2 * tq), lambda s, j, q: (0, q)),
                  pl.BlockSpec((tq, 2), lambda s, j, q: (q, 0)),
                  plane_spec, plane_spec, plane_spec, plane_spec,
                  pl.BlockSpec((1, dt), lambda s, j, q: (0, j))]
                 + [spec for _, row0, col in sources for spec in seq_specs(row0, col)],
        out_specs=[out_spec, out_spec],
        out_shape=[parity, parity],
        scratch_shapes=[pltpu.VMEM((half, nb * dt), BF16), pltpu.VMEM((half, nb * dt), BF16),
                        pltpu.VMEM((half, nb * dt), F32), pltpu.VMEM((half, nb * dt), F32)],
        compiler_params=_params(("parallel", "parallel", "arbitrary"), est),
        name="hyena_conv",
    )(wf, wft, tw, *planes, skip, *[arr for arr, _, _ in sources for _ in range(nb)])


def _rope(x, cos, sin, dh):
    outs = []
    for c in range(2):
        xc = x[:, c * dh:(c + 1) * dh]
        lane = lax.broadcasted_iota(jnp.int32, xc.shape, 1)
        first_half = (lane & 32) == 0
        sw = jnp.where(first_half, pltpu.roll(xc, dh - 32, 1), pltpu.roll(xc, 32, 1))
        outs.append(xc * cos + sw * sin)
    return jnp.concatenate(outs, axis=1)


def _rope_kv_kernel(k_ref, v_ref, cos_ref, sin_ref, ko_ref, vo_ref, *, dh):
    ko_ref[...] = _rope(k_ref[...], cos_ref[...], sin_ref[...], dh).astype(BF16)
    vo_ref[...] = v_ref[...].astype(BF16)


def _rope_kv_call(qkv, cos, sin, tp, d, dh):
    t = qkv.shape[0]
    ts = t - tp
    length = cos.shape[0]
    tr = _divisor(math.gcd(length, tp), 512, HALO)
    nh = d // (2 * dh)
    per_seq = length // tr
    est = 8 * tr * 2 * dh * 4 + 4 * tr * dh * 4
    return pl.pallas_call(
        functools.partial(_rope_kv_kernel, dh=dh),
        grid=(ts // tr, nh),
        in_specs=[pl.BlockSpec((tr, 2 * dh), lambda i, h: (tp // tr + i, nh + h)),
                  pl.BlockSpec((tr, 2 * dh), lambda i, h: (tp // tr + i, 2 * nh + h)),
                  pl.BlockSpec((tr, dh), lambda i, h: (i % per_seq, 0)),
                  pl.BlockSpec((tr, dh), lambda i, h: (i % per_seq, 0))],
        out_specs=[pl.BlockSpec((tr, 2 * dh), lambda i, h: (i, h)),
                   pl.BlockSpec((tr, 2 * dh), lambda i, h: (i, h))],
        out_shape=[jax.ShapeDtypeStruct((ts, d), BF16), jax.ShapeDtypeStruct((ts, d), BF16)],
        compiler_params=_params(("parallel", "parallel"), est),
        name="attn_rope_kv",
    )(qkv, qkv, cos, sin)


def _attn_kernel(*refs, latent, dh, lam_init):
    if latent:
        lam_ref, sub_ref, q_ref, k_ref, v_ref, ck_ref, cv_ref, cos_ref, sin_ref, o_ref = refs
    else:
        lam_ref, sub_ref, q_ref, k_ref, v_ref, o_ref = refs
    scale = dh ** -0.5
    lp = lam_ref[...]
    lam = (jnp.exp(jnp.sum(lp[0:1] * lp[1:2], axis=1, keepdims=True))
           - jnp.exp(jnp.sum(lp[2:3] * lp[3:4], axis=1, keepdims=True)) + lam_init)
    q = q_ref[...]
    if latent:
        q = _rope(q, cos_ref[...], sin_ref[...], dh)
    qb = q.astype(BF16)
    segs = [(k_ref, v_ref)] + ([(ck_ref, cv_ref)] if latent else [])
    chunks = []
    for kr, vr in segs:
        n = kr.shape[0]
        ck = math.gcd(n, KV_CHUNK)
        chunks += [(kr, vr, i * ck, ck) for i in range(n // ck)]
    nt = (((1,), (1,)), ((), ()))
    c2 = scale * math.log2(math.e)
    m = [None, None]
    l = [None, None]
    acc = [None, None]
    for kr, vr, r0, ck in chunks:
        vv = vr[r0:r0 + ck, :].astype(BF16)
        for c in range(2):
            kk = kr[r0:r0 + ck, c * dh:(c + 1) * dh].astype(BF16)
            s = lax.dot_general(qb[:, c * dh:(c + 1) * dh], kk, nt, preferred_element_type=F32)
            mc = jnp.max(s, axis=1, keepdims=True)
            m_new = mc if m[c] is None else jnp.maximum(m[c], mc)
            e = jnp.exp2((s - m_new) * c2)
            es = jnp.sum(e, axis=1, keepdims=True)
            pv = jnp.dot(e.astype(BF16), vv, preferred_element_type=F32)
            if m[c] is None:
                l[c], acc[c] = es, pv
            else:
                shrink = jnp.exp2((m[c] - m_new) * c2)
                l[c] = shrink * l[c] + es
                acc[c] = shrink * acc[c] + pv
            m[c] = m_new
    o = acc[0] * (1.0 / l[0]) - acc[1] * (lam / l[1])
    o = o * lax.rsqrt(jnp.mean(o * o, axis=1, keepdims=True) + SUBLN_EPSILON) * sub_ref[...] * (1.0 - lam_init)
    o_ref[...] = o.astype(o_ref.dtype)


def _attn_ctx_call(qkv, lam_p, subln, n_seq, length, d, dh, lam_init):
    nh = d // (2 * dh)
    tq = _divisor(length, 256, HALO)
    per_seq = length // tq
    est = 2 * (tq + 2 * length) * 2 * dh * 4 + 10 * tq * length * 4 + 4 * length * 2 * dh * 2
    return pl.pallas_call(
        functools.partial(_attn_kernel, latent=False, dh=dh, lam_init=lam_init),
        grid=(n_seq, nh, per_seq),
        in_specs=[pl.BlockSpec((4, dh), lambda b, h, i: (0, 0)),
                  pl.BlockSpec((1, 2 * dh), lambda b, h, i: (0, 0)),
                  pl.BlockSpec((tq, 2 * dh), lambda b, h, i: (b * per_seq + i, h)),
                  pl.BlockSpec((length, 2 * dh), lambda b, h, i: (b, nh + h)),
                  pl.BlockSpec((length, 2 * dh), lambda b, h, i: (b, 2 * nh + h))],
        out_specs=pl.BlockSpec((tq, 2 * dh), lambda b, h, i: (b * per_seq + i, h)),
        out_shape=jax.ShapeDtypeStruct((n_seq * length, d), BF16),
        compiler_params=_params(("parallel", "parallel", "arbitrary"), est),
        name="attn_context",
    )(lam_p, subln.reshape(1, 2 * dh), qkv, qkv, qkv)


def _attn_lat_call(qkv, kr, vb, ck, cv, cos, sin, lam_p, subln, tp, n_seq, length, d, dh, lam_init):
    nh = d // (2 * dh)
    lc = ck.shape[1]
    tq = _divisor(math.gcd(length, tp), 256, HALO)
    per_seq = length // tq
    est = (4 * (length + lc) * 2 * dh * 2 + 4 * lc * 2 * dh * 4 + 4 * tq * 2 * dh * 4
           + 5 * tq * (length + lc) * 4)
    return pl.pallas_call(
        functools.partial(_attn_kernel, latent=True, dh=dh, lam_init=lam_init),
        grid=(n_seq, nh, per_seq),
        in_specs=[pl.BlockSpec((4, dh), lambda b, h, i: (0, 0)),
                  pl.BlockSpec((1, 2 * dh), lambda b, h, i: (0, 0)),
                  pl.BlockSpec((tq, 2 * dh), lambda b, h, i: (tp // tq + b * per_seq + i, h)),
                  pl.BlockSpec((length, 2 * dh), lambda b, h, i: (b, h)),
                  pl.BlockSpec((length, 2 * dh), lambda b, h, i: (b, h)),
                  pl.BlockSpec((None, lc, 2 * dh), lambda b, h, i: (b, 0, h)),
                  pl.BlockSpec((None, lc, 2 * dh), lambda b, h, i: (b, 0, h)),
                  pl.BlockSpec((tq, dh), lambda b, h, i: (i, 0)),
                  pl.BlockSpec((tq, dh), lambda b, h, i: (i, 0))],
        out_specs=pl.BlockSpec((tq, 2 * dh), lambda b, h, i: (b * per_seq + i, h)),
        out_shape=jax.ShapeDtypeStruct((n_seq * length, d), BF16),
        compiler_params=_params(("parallel", "parallel", "arbitrary"), est),
        name="attn_latent",
    )(lam_p, subln.reshape(1, 2 * dh), qkv, kr, vb, ck, cv, cos, sin)


def _rope_tables(length, dh):
    half = dh // 2
    pos = jnp.arange(length)
    pos_r = (pos // GRID_COLS).astype(F32)
    pos_c = (pos % GRID_COLS).astype(F32)
    inv = jnp.power(ROPE_THETA, -jnp.arange(0, half, 2, dtype=F32) / half)
    ang_r = pos_r[:, None] * inv[None]
    ang_c = pos_c[:, None] * inv[None]
    cos = jnp.concatenate([jnp.cos(ang_r)] * 2 + [jnp.cos(ang_c)] * 2, axis=1)
    sin = jnp.concatenate([-jnp.sin(ang_r), jnp.sin(ang_r), -jnp.sin(ang_c), jnp.sin(ang_c)], axis=1)
    return cos, sin


def kernel(x_prompt, x_sample, cache_k, cache_v, c, c_ctx, mod_w, mod_b, ln1_g, ln1_b, ln2_g, ln2_b, pool_w, pool_scale, hyena_w_in, hyena_conv, hyena_ffn_w1, hyena_ffn_b1, hyena_ffn_w2, hyena_ffn_b2, hyena_ffn_w3, hyena_ffn_b3, hyena_freq, hyena_decay, hyena_skip, hyena_w_out, attn_w_qkv, attn_lambda, attn_subln, attn_w_out, ffn_w_up, ffn_conv, ffn_w_down):
    n_ctx, s_len, d = x_prompt.shape
    n_lat, ds, _ = x_sample.shape
    depth = mod_w.shape[0]
    tp, ts = n_ctx * s_len, n_lat * ds
    t = tp + ts
    dims = (tp, s_len, ds)
    alpha = (2 * depth) ** 0.25
    nh = cache_k.shape[3]
    dh = cache_k.shape[4] // 2
    lc = cache_k.shape[2]
    assert tp % ds == 0
    assert n_lat + 1 <= 8

    cond8 = jnp.zeros((8, d), F32).at[0].set(c_ctx).at[1:1 + n_lat].set(c)
    mods = _mods_call(cond8, mod_w, mod_b)
    mods4 = mods.reshape(depth, 8, 1, 6 * d)

    x = jnp.concatenate([x_prompt.reshape(tp, d), x_sample.reshape(ts, d)], axis=0)

    pos = jnp.concatenate([jnp.arange(tp) % s_len, jnp.arange(ts) % ds])
    last = jnp.concatenate([jnp.full((tp,), s_len - 1), jnp.full((ts,), ds - 1)])
    edge = jnp.stack([(pos != 0), (pos != last)], axis=1).astype(F32)

    w_down16 = ffn_w_down.astype(BF16)
    f = ffn_w_down.shape[1]
    w_gate16 = ffn_w_up[:, :, :f].astype(BF16)
    w_up16 = ffn_w_up[:, :, f:].astype(BF16)
    conv_gate, conv_up = ffn_conv[:, :, :f], ffn_conv[:, :, f:]
    hyena_in16 = hyena_w_in.astype(BF16)
    qkv16 = attn_w_qkv.astype(BF16)
    hyena_out16 = hyena_w_out.astype(BF16)
    attn_out16 = attn_w_out.astype(BF16)

    new_k = new_v = None
    h_in = None
    for i in range(depth):
        kind, j = i % N_MIXER_KINDS, i // N_MIXER_KINDS
        if kind == 0:
            x, h2 = _pool_call(x, mods4, i, pool_w[j].astype(BF16), pool_scale[j], ln1_g[i], ln1_b[i], dims, alpha)
        elif kind == 1:
            p = _up_conv_call(h_in, edge, hyena_in16, hyena_conv, j)
            z2s = []
            for length, n_seq, row0 in ((s_len, n_ctx, 0), (ds, n_lat, tp // ds)):
                hk, asum = _filter_call(length, hyena_ffn_w1[j], hyena_ffn_b1[j], hyena_ffn_w2[j], hyena_ffn_b2[j],
                                        hyena_ffn_w3[j], hyena_ffn_b3[j], hyena_freq[j], hyena_decay[j])
                half = length // 2
                tq = _divisor(half, 256, SUBLANES_F32)
                wf, wft = _dft_tables(half, tq)
                phase = jnp.arange(half, dtype=F32) * (math.pi / length)
                tw = jnp.stack([jnp.cos(phase), jnp.sin(phase)], axis=1)
                planes = _kf_call(wf, tw, hk, asum, tq)
                tabs = (wf, wft, tw, tq)
                p6 = p.reshape(t // length, half, 6 * d)
                z1e, z1o = _dftconv_call(tabs, planes, 0, hyena_skip[j, 0:1], (p6, row0, 0), (p6, row0, 3),
                                         (p6, row0, 1), (p6, row0, 4), n_seq, d)
                z2e, z2o = _dftconv_call(tabs, planes, 1, hyena_skip[j, 1:2], (z1e, 0, 0), (z1o, 0, 0),
                                         (p6, row0, 2), (p6, row0, 5), n_seq, d)
                z2s.append(jnp.stack([z2e, z2o], axis=2).reshape(n_seq * length, d))
            z2 = jnp.concatenate(z2s, axis=0).astype(BF16)
            x, h2 = _down_ln_call(z2, hyena_out16, j, x, mods4, i, 2, i, 3, ln1_g[i], ln1_b[i],
                                  dims, alpha, "hyena_out")
        else:
            lam_init = 0.8 - 0.6 * math.exp(-0.3 * i)
            qkv = _up_plain_call(h_in, qkv16, j)
            new_k = qkv[:tp, d:2 * d].reshape(n_ctx, s_len, nh, 2 * dh)
            new_v = qkv[:tp, 2 * d:3 * d].reshape(n_ctx, s_len, nh, 2 * dh)
            o_ctx = _attn_ctx_call(qkv, attn_lambda[j], attn_subln[j], n_ctx, s_len, d, dh, lam_init)
            cos, sin = _rope_tables(ds, dh)
            kr, vb = _rope_kv_call(qkv, cos, sin, tp, d, dh)
            o_lat = _attn_lat_call(qkv, kr, vb, cache_k[:, j].reshape(n_lat, lc, d), cache_v[:, j].reshape(n_lat, lc, d),
                                   cos, sin, attn_lambda[j], attn_subln[j], tp, n_lat, ds, d, dh, lam_init)
            o = jnp.concatenate([o_ctx, o_lat], axis=0)
            x, h2 = _down_ln_call(o, attn_out16, j, x, mods4, i, 2, i, 3, ln1_g[i], ln1_b[i],
                                  dims, alpha, "attn_out")
        a = _up_ffn_call(h2, edge, w_gate16, w_up16, conv_gate, conv_up, i)
        last_layer = i == depth - 1
        x, h_in = _down_ln_call(a, w_down16, i, x, mods4, i, 5, i + 1, None if last_layer else 0,
                                ln2_g[i], ln2_b[i], dims, alpha, "ffn_down")

    y_prompt = x[:tp].reshape(n_ctx, s_len, d)
    y_sample = x[tp:].reshape(n_lat, ds, d)
    return (y_prompt, y_sample, new_k[:, None], new_v[:, None])
```

```python
import functools
import math

import jax
import jax.numpy as jnp
from jax import lax
from jax.experimental import pallas as pl
from jax.experimental.pallas import tpu as pltpu

F32 = jnp.float32
BF16 = jnp.bfloat16

GRID_COLS = 64
POOL_WINDOW_SIZES = (2, 4, 8, 16)
ROPE_THETA = 10000.0
LN_EPSILON = 1e-6
SUBLN_EPSILON = 1e-5
N_MIXER_KINDS = 3

V7X_VMEM_BYTES = 64 * 1024 * 1024
V7X_VMEM_BUDGET = V7X_VMEM_BYTES - 6 * 1024 * 1024
SUBLANES_F32 = 8
SUBLANES_BF16 = 16
LANES = 128
KV_CHUNK = 1024
LN_CHUNK_ROWS = 64


def _params(sem, est_bytes):
    limit = int(min(V7X_VMEM_BUDGET, max(32 * 1024 * 1024, est_bytes * 5 // 4)))
    return pltpu.CompilerParams(dimension_semantics=sem, vmem_limit_bytes=limit)


def _divisor(n, pref, mult):
    best = None
    d = mult
    while d <= min(n, pref):
        if n % d == 0:
            best = d
        d += mult
    assert best is not None, (n, pref, mult)
    return best


def _layer_norm(r, g, b):
    mu = jnp.mean(r, -1, keepdims=True)
    d = r - mu
    var = jnp.mean(d * d, -1, keepdims=True)
    return d * lax.rsqrt(var + LN_EPSILON) * g + b


def _mod_row(t0, tp, ds):
    return jnp.where(t0 < tp, 0, 1 + (t0 - tp) // ds)


def _mods_kernel(c_ref, w_ref, b_ref, o_ref):
    k = pl.program_id(2)

    @pl.when(k == 0)
    def _():
        o_ref[0] = jnp.broadcast_to(b_ref[0], o_ref.shape[1:])

    c = c_ref[...]
    s = (c * jax.nn.sigmoid(c)).astype(BF16)
    o_ref[0] += jnp.dot(s, w_ref[0].astype(BF16), preferred_element_type=F32)


def _mods_call(cond8, mod_w, mod_b):
    depth, d, n = mod_w.shape
    tk = _divisor(d, 2048, LANES)
    tn = _divisor(n, 1024, LANES)
    est = 2 * tk * tn * 4 + 4 * 8 * tn * 4 + 2 * 8 * tk * 4 + tk * tn * 2
    return pl.pallas_call(
        _mods_kernel,
        grid=(depth, n // tn, d // tk),
        in_specs=[pl.BlockSpec((8, tk), lambda l, j, k: (0, k)),
                  pl.BlockSpec((1, tk, tn), lambda l, j, k: (l, k, j)),
                  pl.BlockSpec((1, 1, tn), lambda l, j, k: (l, 0, j))],
        out_specs=pl.BlockSpec((1, 8, tn), lambda l, j, k: (l, 0, j)),
        out_shape=jax.ShapeDtypeStruct((depth, 8, n), F32),
        compiler_params=_params(("parallel", "parallel", "arbitrary"), est),
        name="mods",
    )(cond8, mod_w, mod_b.reshape(depth, 1, n))


def _pool_kernel(xm_ref, xp_ref, xn_ref, mod_ref, pw_ref, ps_ref, lg_ref, lb_ref, xo_ref, ho_ref, e_ref,
                 *, tm, tp, s_len, ds, d, alpha):
    t0 = pl.program_id(0) * tm
    is_ctx = t0 < tp
    seq_len = jnp.where(is_ctx, s_len, ds)
    pos0 = jnp.where(is_ctx, t0 % s_len, (t0 - tp) % ds)
    mods = mod_ref[...]
    sh1, sc1, g1 = mods[:, 0:d], mods[:, d:2 * d], mods[:, 2 * d:3 * d]
    sh2, sc2 = mods[:, 3 * d:4 * d], mods[:, 4 * d:5 * d]
    x = xm_ref[...]
    h = x * (1.0 + sc1) + sh1
    has_prev = pos0 > 0
    has_next = pos0 + tm < seq_len
    e_ref[0:8, :] = jnp.where(has_prev, xp_ref[...] * (1.0 + sc1) + sh1, 0.0)
    e_ref[8:8 + tm, :] = h
    e_ref[8 + tm:16 + tm, :] = jnp.where(has_next, xn_ref[...] * (1.0 + sc1) + sh1, 0.0)
    t = pos0 + lax.broadcasted_iota(jnp.int32, (tm, 1), 0)
    cg = d // len(POOL_WINDOW_SIZES)
    ys = []
    for g, w in enumerate(POOL_WINDOW_SIZES):
        c0 = g * cg
        acc = None
        for s in range(-(w // 2), w - w // 2):
            v = e_ref[8 + s:8 + s + tm, c0:c0 + cg]
            acc = v if acc is None else acc + v
        lo = jnp.maximum(t - w // 2, 0)
        hi = jnp.minimum(t + (w - w // 2), seq_len)
        inv = 1.0 / (hi - lo).astype(F32)
        p = acc * inv - h[:, c0:c0 + cg]
        ys.append(jnp.dot(p.astype(BF16), pw_ref[g], preferred_element_type=F32))
    y = jnp.concatenate(ys, axis=1) * ps_ref[...]
    xn = _layer_norm(alpha * x + g1 * y, lg_ref[...], lb_ref[...])
    xo_ref[...] = xn
    ho_ref[...] = (xn * (1.0 + sc2) + sh2).astype(BF16)


def _pool_call(x, mods4, layer, pw, ps, lg, lb, dims, alpha):
    t, d = x.shape
    tp, s_len, ds = dims
    tm = s_len
    assert ds % tm == 0 and tm % SUBLANES_BF16 == 0
    n8 = t // 8
    kern = functools.partial(_pool_kernel, tm=tm, tp=tp, s_len=s_len, ds=ds, d=d, alpha=alpha)
    g, cg, _ = pw.shape
    est = (2 * tm * d * 4 * 2 + 2 * tm * d * 2 + (tm + 16) * d * 4 + 2 * g * cg * cg * 2
           + 6 * tm * d * 4)
    return pl.pallas_call(
        kern,
        grid=(t // tm,),
        in_specs=[pl.BlockSpec((tm, d), lambda i: (i, 0)),
                  pl.BlockSpec((8, d), lambda i: (jnp.maximum(i * (tm // 8) - 1, 0), 0)),
                  pl.BlockSpec((8, d), lambda i: (jnp.minimum((i + 1) * (tm // 8), n8 - 1), 0)),
                  pl.BlockSpec((None, None, 1, 6 * d), lambda i: (layer, _mod_row(i * tm, tp, ds), 0, 0)),
                  pl.BlockSpec((g, cg, cg), lambda i: (0, 0, 0)),
                  pl.BlockSpec((1, d), lambda i: (0, 0)),
                  pl.BlockSpec((1, d), lambda i: (0, 0)),
                  pl.BlockSpec((1, d), lambda i: (0, 0))],
        out_specs=[pl.BlockSpec((tm, d), lambda i: (i, 0)),
                   pl.BlockSpec((tm, d), lambda i: (i, 0))],
        out_shape=[jax.ShapeDtypeStruct((t, d), F32), jax.ShapeDtypeStruct((t, d), BF16)],
        scratch_shapes=[pltpu.VMEM((tm + 16, d), F32)],
        compiler_params=_params(("parallel",), est),
        name="pool_mixer",
    )(x, x, x, mods4, pw, ps.reshape(1, d), lg.reshape(1, d), lb.reshape(1, d))


HALO = SUBLANES_BF16


def _fill_ext(ext_ref, a_ref, ap_ref, an_ref, tm):
    @pl.when(pl.program_id(1) == 0)
    def _():
        ext_ref[0:HALO, :] = ap_ref[...]
        ext_ref[HALO:HALO + tm, :] = a_ref[...]
        ext_ref[HALO + tm:2 * HALO + tm, :] = an_ref[...]


def _conv3(p, cw, not_first, not_last, tm):
    rows = p.shape[0]
    up = pltpu.roll(p, 1, 0)[HALO:HALO + tm]
    dn = pltpu.roll(p, rows - 1, 0)[HALO:HALO + tm]
    mid = p[HALO:HALO + tm]
    up = jnp.where(not_first > 0.0, up, 0.0)
    dn = jnp.where(not_last > 0.0, dn, 0.0)
    return cw[0:1] * up + cw[1:2] * mid + cw[2:3] * dn


def _up_ffn_kernel(a_ref, ap_ref, an_ref, edge_ref, wg_ref, wu_ref, cg_ref, cu_ref, o_ref, ext_ref, *, tm):
    _fill_ext(ext_ref, a_ref, ap_ref, an_ref, tm)
    e = edge_ref[...]
    nf, nl = e[:, 0:1], e[:, 1:2]
    a = ext_ref[...]
    g = _conv3(jnp.dot(a, wg_ref[...], preferred_element_type=F32), cg_ref[...], nf, nl, tm)
    u = _conv3(jnp.dot(a, wu_ref[...], preferred_element_type=F32), cu_ref[...], nf, nl, tm)
    o_ref[...] = (g * jax.nn.sigmoid(g) * u).astype(o_ref.dtype)


def _up_conv_kernel(a_ref, ap_ref, an_ref, edge_ref, w_ref, cw_ref, oe_ref, oo_ref, ext_ref, par_ref, *, tm):
    _fill_ext(ext_ref, a_ref, ap_ref, an_ref, tm)
    e = edge_ref[...]
    p = jnp.dot(ext_ref[...], w_ref[...], preferred_element_type=F32)
    c = _conv3(p, cw_ref[...], e[:, 0:1], e[:, 1:2], tm)
    for k in range(c.shape[1] // LANES):
        cols = slice(k * LANES, (k + 1) * LANES)
        par_ref[k] = c[:, cols]
        oe_ref[:, cols] = par_ref[k, pl.ds(0, tm // 2, stride=2), :]
        oo_ref[:, cols] = par_ref[k, pl.ds(1, tm // 2, stride=2), :]


def _up_plain_kernel(a_ref, w_ref, o_ref):
    o_ref[...] = jnp.dot(a_ref[...], w_ref[...], preferred_element_type=F32).astype(o_ref.dtype)


def _halo_specs(tm, d, t):
    nh = t // HALO
    return [pl.BlockSpec((tm, d), lambda i, j: (i, 0)),
            pl.BlockSpec((HALO, d), lambda i, j: (jnp.maximum(i * (tm // HALO) - 1, 0), 0)),
            pl.BlockSpec((HALO, d), lambda i, j: (jnp.minimum((i + 1) * (tm // HALO), nh - 1), 0)),
            pl.BlockSpec((tm, 2), lambda i, j: (i, 0))]


def _up_ffn_call(h, edge, wg, wu, cg, cu, layer):
    t, d = h.shape
    f = wg.shape[2]
    tm = _divisor(t, 1024, HALO)
    tn = min(512, f)
    est = (2 * tm * d * 2 + (tm + 2 * HALO) * d * 2 + 4 * d * tn * 2 + 2 * tm * tn * 2
           + 8 * (tm + 2 * HALO) * tn * 4)
    return pl.pallas_call(
        functools.partial(_up_ffn_kernel, tm=tm),
        grid=(t // tm, pl.cdiv(f, tn)),
        in_specs=_halo_specs(tm, d, t) + [
            pl.BlockSpec((None, d, tn), lambda i, j: (layer, 0, j)),
            pl.BlockSpec((None, d, tn), lambda i, j: (layer, 0, j)),
            pl.BlockSpec((None, 3, tn), lambda i, j: (layer, 0, j)),
            pl.BlockSpec((None, 3, tn), lambda i, j: (layer, 0, j))],
        out_specs=pl.BlockSpec((tm, tn), lambda i, j: (i, j)),
        out_shape=jax.ShapeDtypeStruct((t, f), BF16),
        scratch_shapes=[pltpu.VMEM((tm + 2 * HALO, d), BF16)],
        compiler_params=_params(("parallel", "arbitrary"), est),
        name="ffn_up",
    )(h, h, h, edge, wg, wu, cg, cu)


def _up_conv_call(h, edge, w, cw, layer):
    t, d = h.shape
    n = w.shape[2]
    tm = _divisor(t, 1024, HALO)
    tn = _divisor(n, 512, LANES)
    est = (2 * tm * d * 2 + (tm + 2 * HALO) * d * 2 + 2 * d * tn * 2 + 2 * tm * tn * 4
           + 5 * (tm + 2 * HALO) * tn * 4)
    return pl.pallas_call(
        functools.partial(_up_conv_kernel, tm=tm),
        grid=(t // tm, n // tn),
        in_specs=_halo_specs(tm, d, t) + [
            pl.BlockSpec((None, d, tn), lambda i, j: (layer, 0, j)),
            pl.BlockSpec((None, 3, tn), lambda i, j: (layer, 0, j))],
        out_specs=[pl.BlockSpec((tm // 2, tn), lambda i, j: (i, j))] * 2,
        out_shape=[jax.ShapeDtypeStruct((t // 2, n), F32)] * 2,
        scratch_shapes=[pltpu.VMEM((tm + 2 * HALO, d), BF16), pltpu.VMEM((tn // LANES, tm, LANES), F32)],
        compiler_params=_params(("parallel", "arbitrary"), est),
        name="hyena_in",
    )(h, h, h, edge, w, cw)


def _up_plain_call(h, w, layer):
    t, d = h.shape
    n = w.shape[2]
    tm = _divisor(t, 1024, HALO)
    tn = _divisor(n, 512, LANES)
    est = 2 * tm * d * 2 + 2 * d * tn * 2 + 2 * tm * tn * 4 + 2 * tm * tn * 4
    return pl.pallas_call(
        _up_plain_kernel,
        grid=(t // tm, n // tn),
        in_specs=[pl.BlockSpec((tm, d), lambda i, j: (i, 0)),
                  pl.BlockSpec((None, d, tn), lambda i, j: (layer, 0, j))],
        out_specs=pl.BlockSpec((tm, tn), lambda i, j: (i, j)),
        out_shape=jax.ShapeDtypeStruct((t, n), F32),
        compiler_params=_params(("parallel", "arbitrary"), est),
        name="attn_qkv",
    )(h, w)


def _down_ln_kernel(*refs, nj, tn, d, gate, alpha, nxt):
    if nxt is None:
        a_ref, w_ref, x_ref, mod_ref, lg_ref, lb_ref, xo_ref = refs
    else:
        a_ref, w_ref, x_ref, mod_ref, modn_ref, lg_ref, lb_ref, xo_ref, ho_ref = refs
    j = pl.program_id(1)
    col = pl.multiple_of(j * tn, tn)
    xo_ref[:, pl.ds(col, tn)] = jnp.dot(a_ref[...], w_ref[...], preferred_element_type=F32)

    @pl.when(j == nj - 1)
    def _():
        g = mod_ref[:, gate * d:(gate + 1) * d]
        tm = xo_ref.shape[0]
        ch = math.gcd(tm, LN_CHUNK_ROWS)

        def chunk(r, carry):
            rows = pl.ds(pl.multiple_of(r * ch, ch), ch)
            xn = _layer_norm(alpha * x_ref[rows, :] + g * xo_ref[rows, :], lg_ref[...], lb_ref[...])
            xo_ref[rows, :] = xn
            if nxt is not None:
                sh = modn_ref[:, nxt * d:(nxt + 1) * d]
                sc = modn_ref[:, (nxt + 1) * d:(nxt + 2) * d]
                ho_ref[rows, :] = (xn * (1.0 + sc) + sh).astype(BF16)
            return carry

        lax.fori_loop(0, tm // ch, chunk, 0)


def _down_ln_call(a, w, wlayer, x, mods4, layer, gate, nxt_layer, nxt, lg, lb, dims, alpha, name):
    t, kdim = a.shape
    d = w.shape[2]
    tp, s_len, ds = dims
    tm = _divisor(math.gcd(tp, ds), 512, HALO)
    tn = _divisor(d, 512, LANES)
    nj = d // tn
    kern = functools.partial(_down_ln_kernel, nj=nj, tn=tn, d=d, gate=gate, alpha=alpha, nxt=nxt)
    fixed = 2 * kdim * tn * 2 + tm * d * 4 + 2 * tm * tn * 4 + (4 << 20)
    per_tile = tm * kdim * 2 + tm * d * 4 + (tm * d * 2 if nxt is not None else 0)
    once = pl.Buffered(1)
    rest = pl.Buffered(2) if fixed + 2 * per_tile <= V7X_VMEM_BUDGET else once
    mod_spec = lambda l: pl.BlockSpec((None, None, 1, 6 * d),
                                      lambda i, k: (l, _mod_row(i * tm, tp, ds), 0, 0))
    in_specs = [pl.BlockSpec((tm, kdim), lambda i, k: (i, 0), pipeline_mode=rest),
                pl.BlockSpec((None, kdim, tn), lambda i, k: (wlayer, 0, k)),
                pl.BlockSpec((tm, d), lambda i, k: (i, 0), pipeline_mode=once),
                mod_spec(layer)]
    args = [a, w, x, mods4]
    if nxt is not None:
        in_specs.append(mod_spec(nxt_layer))
        args.append(mods4)
    in_specs += [pl.BlockSpec((1, d), lambda i, k: (0, 0)), pl.BlockSpec((1, d), lambda i, k: (0, 0))]
    args += [lg.reshape(1, d), lb.reshape(1, d)]
    out_specs = [pl.BlockSpec((tm, d), lambda i, k: (i, 0), pipeline_mode=rest)]
    out_shape = [jax.ShapeDtypeStruct((t, d), F32)]
    if nxt is not None:
        out_specs.append(pl.BlockSpec((tm, d), lambda i, k: (i, 0), pipeline_mode=rest))
        out_shape.append(jax.ShapeDtypeStruct((t, d), BF16))
    est = fixed + (2 if rest is not once else 1) * per_tile
    res = pl.pallas_call(
        kern,
        grid=(t // tm, nj),
        in_specs=in_specs,
        out_specs=out_specs,
        out_shape=out_shape,
        compiler_params=_params(("parallel", "arbitrary"), est),
        name=name,
    )(*args)
    return (res[0], res[1]) if nxt is not None else (res[0], None)


def _dot_f32(a, b):
    return lax.dot_general(a, b, (((1,), (0,)), ((), ())), precision=lax.Precision.HIGHEST,
                           preferred_element_type=F32)


def _filter_kernel(z_ref, tu_ref, w1_ref, b1_ref, w2_ref, b2_ref, w3_ref, b3_ref, fr_ref, dec_ref,
                   hk_ref, as_ref, *, tl, n_col_tiles):
    jc = pl.program_id(0)
    i = pl.program_id(1)
    h = jnp.sin(fr_ref[0:1, :] * (_dot_f32(z_ref[...], w1_ref[...]) + b1_ref[...]))
    h = jnp.sin(fr_ref[1:2, :] * (_dot_f32(h, w2_ref[...]) + b2_ref[...]))
    h = jnp.dot(h.astype(BF16), w3_ref[...].astype(BF16), preferred_element_type=F32) + b3_ref[...]
    h = h * jnp.exp(-tu_ref[...] * jnp.abs(dec_ref[...]))
    row = i * tl + lax.broadcasted_iota(jnp.int32, (tl, 1), 0)
    drop = jnp.logical_and(row == 0, jc >= n_col_tiles // 2)
    h = jnp.where(drop, 0.0, h)
    hk_ref[...] = h.astype(BF16)
    part = jnp.sum(jnp.abs(h), axis=0, keepdims=True)

    @pl.when(i == 0)
    def _():
        as_ref[...] = part

    @pl.when(i > 0)
    def _():
        as_ref[...] += part


def _filter_call(length, w1, b1, w2, b2, w3, b3, freq, decay):
    emb, width = w1.shape
    n = w3.shape[1]
    t = jnp.arange(length).reshape(length // 2, 2).T.reshape(length).astype(F32)
    t_unit = t / max(length - 1, 1)
    n_bands = (emb - 1) // 2
    bands = jnp.linspace(1e-4, n_bands - 1, n_bands, dtype=F32)
    ang = (2.0 * math.pi / length) * t[:, None] * bands[None, :]
    z = jnp.concatenate([t_unit[:, None], jnp.cos(ang), -jnp.sin(ang)], -1)
    z = jnp.pad(z, ((0, 0), (0, LANES - emb)))
    w1p = jnp.pad(w1, ((0, LANES - emb), (0, 0)))
    tl = _divisor(length, 256, 8)
    tn = _divisor(n // 2, 2048, LANES)
    nct = n // tn
    est = 2 * tl * tn * 2 + 6 * tl * tn * 4 + 4 * width * tn * 4
    return pl.pallas_call(
        functools.partial(_filter_kernel, tl=tl, n_col_tiles=nct),
        grid=(nct, length // tl),
        in_specs=[pl.BlockSpec((tl, LANES), lambda j, i: (i, 0)),
                  pl.BlockSpec((tl, 1), lambda j, i: (i, 0)),
                  pl.BlockSpec((LANES, width), lambda j, i: (0, 0)),
                  pl.BlockSpec((1, width), lambda j, i: (0, 0)),
                  pl.BlockSpec((width, width), lambda j, i: (0, 0)),
                  pl.BlockSpec((1, width), lambda j, i: (0, 0)),
                  pl.BlockSpec((width, tn), lambda j, i: (0, j)),
                  pl.BlockSpec((1, tn), lambda j, i: (0, j)),
                  pl.BlockSpec((2, width), lambda j, i: (0, 0)),
                  pl.BlockSpec((1, tn), lambda j, i: (0, j))],
        out_specs=[pl.BlockSpec((tl, tn), lambda j, i: (i, j)),
                   pl.BlockSpec((1, tn), lambda j, i: (0, j))],
        out_shape=[jax.ShapeDtypeStruct((length, n), BF16), jax.ShapeDtypeStruct((1, n), F32)],
        compiler_params=_params(("parallel", "arbitrary"), est),
        name="hyena_filter",
    )(z, t_unit[:, None], w1p, b1.reshape(1, width), w2, b2.reshape(1, width), w3, b3.reshape(1, n),
      freq, decay.reshape(1, n))


def _dft_tables(length, tq):
    n2 = 2 * length
    nq = length // tq
    n = jnp.arange(length, dtype=jnp.int32)
    pa = (jnp.arange(nq, dtype=jnp.int32)[:, None] * tq * n[None, :]) % n2
    pb = (jnp.arange(tq, dtype=jnp.int32)[:, None] * n[None, :]) % n2
    a = pa.astype(F32) * (2.0 * math.pi / n2)
    b = pb.astype(F32) * (2.0 * math.pi / n2)
    ca, sa = jnp.cos(a)[:, None, :], jnp.sin(a)[:, None, :]
    cb, sb = jnp.cos(b)[None, :, :], jnp.sin(b)[None, :, :]
    cos_rows = ca * cb - sa * sb
    sin_rows = -(sa * cb + ca * sb)
    nyq = jnp.where(n % 2 == 0, 1.0, -1.0)[None, None, :]
    first = jnp.logical_and(lax.broadcasted_iota(jnp.int32, (nq, tq, 1), 0) == 0,
                            lax.broadcasted_iota(jnp.int32, (nq, tq, 1), 1) == 0)
    sin_rows = jnp.where(first, nyq, sin_rows)
    wf = jnp.concatenate([cos_rows, sin_rows], axis=1).reshape(2 * length, length).astype(BF16)
    return wf, wf.T


def _fold_fwd(e, o, c, s, f0, tq):
    ere, eim, ore, oim = e[:tq], e[tq:], o[:tq], o[tq:]
    tre = ore * c + oim * s
    tim = oim * c - ore * s
    pre = ere + tre
    pim = jnp.where(f0, ere - ore, eim + tim)
    qre = jnp.where(f0, eim, ere - tre)
    qim = jnp.where(f0, -oim, tim - eim)
    return pre, pim, qre, qim


def _kf_kernel(wf_ref, fe_ref, fo_ref, be_ref, bo_ref, af_ref, ab_ref, tw_ref,
               kpre_ref, kpim_ref, kqre_ref, kqim_ref, *, tq, n2):
    q = pl.program_id(1)
    w = wf_ref[...]
    dot = lambda r: jnp.dot(w, r[...], preferred_element_type=F32)
    f0 = (q * tq + lax.broadcasted_iota(jnp.int32, (tq, 1), 0)) == 0
    c, s = tw_ref[:, 0:1], tw_ref[:, 1:2]
    fpre, fpim, fqre, fqim = _fold_fwd(dot(fe_ref), dot(fo_ref), c, s, f0, tq)
    bpre, bpim, bqre, bqim = _fold_fwd(dot(be_ref), dot(bo_ref), c, s, f0, tq)
    inv = 1.0 / (af_ref[...] + ab_ref[...] + 1e-6)
    wq = (2.0 / n2) * inv
    wp = jnp.where(f0, 1.0 / n2, 2.0 / n2) * inv
    kpre_ref[...] = (fpre + bpre) * wp
    kpim_ref[...] = jnp.where(f0, fpim + bpim, fpim - bpim) * wp
    kqre_ref[...] = (fqre + bqre) * wq
    kqim_ref[...] = (fqim - bqim) * wq


def _kf_call(wf, tw, hk, asum, tq):
    half = hk.shape[0] // 2
    n = hk.shape[1] // 2
    dt = _divisor(n, 512, LANES)
    nq = half // tq
    nb = n // dt
    est = 2 * 2 * tq * half * 2 + 8 * half * dt * 2 + 8 * tq * dt * 4 + 24 * tq * dt * 4
    plane = jax.ShapeDtypeStruct((half, n), F32)
    taps = lambda parity, col0: pl.BlockSpec((half, dt), lambda j, q: (parity, col0 + j))
    return pl.pallas_call(
        functools.partial(_kf_kernel, tq=tq, n2=4 * half),
        grid=(nb, nq),
        in_specs=[pl.BlockSpec((2 * tq, half), lambda j, q: (q, 0)),
                  taps(0, 0), taps(1, 0), taps(0, nb), taps(1, nb),
                  pl.BlockSpec((1, dt), lambda j, q: (0, j)),
                  pl.BlockSpec((1, dt), lambda j, q: (0, nb + j)),
                  pl.BlockSpec((tq, 2), lambda j, q: (q, 0))],
        out_specs=[pl.BlockSpec((tq, dt), lambda j, q: (q, j))] * 4,
        out_shape=[plane] * 4,
        compiler_params=_params(("parallel", "arbitrary"), est),
        name="hyena_filter_dft",
    )(wf, hk, hk, hk, hk, asum, asum, tw)


def _dftconv_kernel(*refs, nb, nq, tq, dt, interleave):
    wf_ref, wft_ref, tw_ref, kpre_ref, kpim_ref, kqre_ref, kqim_ref, skip_ref = refs[:8]
    seq = lambda k: refs[8 + k * nb:8 + (k + 1) * nb]
    ue_in, uo_in, ge_in, go_in = seq(0), seq(1), seq(2), seq(3)
    n_out = 1 if interleave else 2
    outs = refs[8 + 4 * nb:8 + 4 * nb + n_out]
    ue_ref, uo_ref, ae_ref, ao_ref = refs[8 + 4 * nb + n_out:12 + 4 * nb + n_out]
    q = pl.program_id(2)
    half = wft_ref.shape[0]

    @pl.when(q == 0)
    def _():
        for b in range(nb):
            cols = slice(b * dt, (b + 1) * dt)
            ue = ue_in[b][...]
            uo = uo_in[b][...]
            ue_ref[:, cols] = ue.astype(BF16)
            uo_ref[:, cols] = uo.astype(BF16)
            ae_ref[:, cols] = ue * skip_ref[...]
            ao_ref[:, cols] = uo * skip_ref[...]

    w = wf_ref[...]
    f0 = (q * tq + lax.broadcasted_iota(jnp.int32, (tq, 1), 0)) == 0
    c, s = tw_ref[:, 0:1], tw_ref[:, 1:2]
    pre, pim, qre, qim = _fold_fwd(jnp.dot(w, ue_ref[...], preferred_element_type=F32),
                                   jnp.dot(w, uo_ref[...], preferred_element_type=F32), c, s, f0, tq)
    wide = lambda r: jnp.concatenate([r[...]] * nb, axis=1)
    kpre, kpim, kqre, kqim = wide(kpre_ref), wide(kpim_ref), wide(kqre_ref), wide(kqim_ref)
    ypre = pre * kpre - jnp.where(f0, 0.0, pim * kpim)
    ypim = jnp.where(f0, pim * kpim, pre * kpim + pim * kpre)
    yqre = qre * kqre - qim * kqim
    yqim = qre * kqim + qim * kqre
    dre, dim = ypre - yqre, ypim + yqim
    z0re = jnp.where(f0, ypre + ypim, ypre + yqre)
    z0im = jnp.where(f0, yqre, ypim - yqim)
    z1re = jnp.where(f0, ypre - ypim, dre * c - dim * s)
    z1im = jnp.where(f0, -yqim, dre * s + dim * c)
    wt = wft_ref[...]
    ae_ref[...] += jnp.dot(wt, jnp.concatenate([z0re, z0im], axis=0).astype(BF16), preferred_element_type=F32)
    ao_ref[...] += jnp.dot(wt, jnp.concatenate([z1re, z1im], axis=0).astype(BF16), preferred_element_type=F32)

    @pl.when(q == nq - 1)
    def _():
        for b in range(nb):
            cols = slice(b * dt, (b + 1) * dt)
            ye = ge_in[b][...] * ae_ref[:, cols]
            yo = go_in[b][...] * ao_ref[:, cols]
            if not interleave:
                outs[0][b] = ye
                outs[1][b] = yo
                continue
            mix_ref = refs[12 + 4 * nb + n_out]
            for k in range(dt // LANES):
                lanes = slice(k * LANES, (k + 1) * LANES)
                mix_ref[pl.ds(0, half, stride=2), :] = ye[:, lanes]
                mix_ref[pl.ds(1, half, stride=2), :] = yo[:, lanes]
                outs[0][b, :, lanes] = mix_ref[...].astype(outs[0].dtype)


def _dftconv_call(tabs, planes, korder, skip, u_even, u_odd, g_even, g_odd, n_seq, d, interleave):
    wf, wft, tw, tq = tabs
    half = wft.shape[0]
    nq = half // tq
    long_seq = half > 512
    dt = _divisor(d, 256 if long_seq else 512, LANES)
    nb = _divisor(n_seq, 2 if long_seq else 4, 1)
    nd = d // dt
    once = pl.Buffered(1)
    blk = half * dt
    est = (4 * 2 * tq * half * 2 + 6 * nb * blk * 4 + 2 * nb * blk * 2 + 2 * nb * blk * 4
           + 2 * nb * blk * 4 + 40 * tq * nb * dt * 4)
    kern = functools.partial(_dftconv_kernel, nb=nb, nq=nq, tq=tq, dt=dt, interleave=interleave)
    sources = (u_even, u_odd, g_even, g_odd)
    scratch = [pltpu.VMEM((half, nb * dt), BF16), pltpu.VMEM((half, nb * dt), BF16),
               pltpu.VMEM((half, nb * dt), F32), pltpu.VMEM((half, nb * dt), F32)]
    if interleave:
        out_specs = [pl.BlockSpec((nb, 2 * half, dt), lambda s, j, q: (s, 0, j), pipeline_mode=once)]
        out_shape = [jax.ShapeDtypeStruct((n_seq, 2 * half, d), BF16)]
        scratch.append(pltpu.VMEM((2 * half, LANES), F32))
    else:
        out_specs = [pl.BlockSpec((nb, half, dt), lambda s, j, q: (s, 0, j), pipeline_mode=once)] * 2
        out_shape = [jax.ShapeDtypeStruct((n_seq, half, d), F32)] * 2
    seq_specs = lambda row0, col: [
        pl.BlockSpec((None, half, dt), lambda s, j, q, b=b: (row0 + s * nb + b, 0, col * nd + j),
                     pipeline_mode=once) for b in range(nb)]
    plane_spec = pl.BlockSpec((tq, dt), lambda s, j, q: (q, korder * nd + j))
    return pl.pallas_call(
        kern,
        grid=(n_seq // nb, nd, nq),
        in_specs=[pl.BlockSpec((2 * tq, half), lambda s, j, q: (q, 0)),
                  pl.BlockSpec((half, 2 * tq), lambda s, j, q: (0, q)),
                  pl.BlockSpec((tq, 2), lambda s, j, q: (q, 0)),
                  plane_spec, plane_spec, plane_spec, plane_spec,
                  pl.BlockSpec((1, dt), lambda s, j, q: (0, j))]
                 + [spec for _, row0, col in sources for spec in seq_specs(row0, col)],
        out_specs=out_specs,
        out_shape=out_shape,
        scratch_shapes=scratch,
        compiler_params=_params(("parallel", "parallel", "arbitrary"), est),
        name="hyena_conv",
    )(wf, wft, tw, *planes, skip, *[arr for arr, _, _ in sources for _ in range(nb)])


def _rope(x, cos, sin, dh):
    outs = []
    for c in range(2):
        xc = x[:, c * dh:(c + 1) * dh]
        lane = lax.broadcasted_iota(jnp.int32, xc.shape, 1)
        first_half = (lane & 32) == 0
        sw = jnp.where(first_half, pltpu.roll(xc, dh - 32, 1), pltpu.roll(xc, 32, 1))
        outs.append(xc * cos + sw * sin)
    return jnp.concatenate(outs, axis=1)


def _rope_kv_kernel(k_ref, v_ref, cos_ref, sin_ref, ko_ref, vo_ref, *, dh):
    ko_ref[...] = _rope(k_ref[...], cos_ref[...], sin_ref[...], dh).astype(BF16)
    vo_ref[...] = v_ref[...].astype(BF16)


def _rope_kv_call(qkv, cos, sin, tp, d, dh):
    t = qkv.shape[0]
    ts = t - tp
    length = cos.shape[0]
    tr = _divisor(math.gcd(length, tp), 512, HALO)
    nh = d // (2 * dh)
    per_seq = length // tr
    est = 8 * tr * 2 * dh * 4 + 4 * tr * dh * 4
    return pl.pallas_call(
        functools.partial(_rope_kv_kernel, dh=dh),
        grid=(ts // tr, nh),
        in_specs=[pl.BlockSpec((tr, 2 * dh), lambda i, h: (tp // tr + i, nh + h)),
                  pl.BlockSpec((tr, 2 * dh), lambda i, h: (tp // tr + i, 2 * nh + h)),
                  pl.BlockSpec((tr, dh), lambda i, h: (i % per_seq, 0)),
                  pl.BlockSpec((tr, dh), lambda i, h: (i % per_seq, 0))],
        out_specs=[pl.BlockSpec((tr, 2 * dh), lambda i, h: (i, h)),
                   pl.BlockSpec((tr, 2 * dh), lambda i, h: (i, h))],
        out_shape=[jax.ShapeDtypeStruct((ts, d), BF16), jax.ShapeDtypeStruct((ts, d), BF16)],
        compiler_params=_params(("parallel", "parallel"), est),
        name="attn_rope_kv",
    )(qkv, qkv, cos, sin)


def _attn_kernel(*refs, latent, dh, lam_init):
    if latent:
        lam_ref, sub_ref, q_ref, k_ref, v_ref, ck_ref, cv_ref, cos_ref, sin_ref, o_ref = refs
    else:
        lam_ref, sub_ref, q_ref, k_ref, v_ref, o_ref = refs
    scale = dh ** -0.5
    lp = lam_ref[...]
    lam = (jnp.exp(jnp.sum(lp[0:1] * lp[1:2], axis=1, keepdims=True))
           - jnp.exp(jnp.sum(lp[2:3] * lp[3:4], axis=1, keepdims=True)) + lam_init)
    q = q_ref[...]
    if latent:
        q = _rope(q, cos_ref[...], sin_ref[...], dh)
    qb = q.astype(BF16)
    segs = [(k_ref, v_ref)] + ([(ck_ref, cv_ref)] if latent else [])
    chunks = []
    for kr, vr in segs:
        n = kr.shape[0]
        ck = math.gcd(n, KV_CHUNK)
        chunks += [(kr, vr, i * ck, ck) for i in range(n // ck)]
    nt = (((1,), (1,)), ((), ()))
    c2 = scale * math.log2(math.e)
    m = [None, None]
    l = [None, None]
    acc = [None, None]
    for kr, vr, r0, ck in chunks:
        vv = vr[r0:r0 + ck, :].astype(BF16)
        for c in range(2):
            kk = kr[r0:r0 + ck, c * dh:(c + 1) * dh].astype(BF16)
            s = lax.dot_general(qb[:, c * dh:(c + 1) * dh], kk, nt, preferred_element_type=F32)
            mc = jnp.max(s, axis=1, keepdims=True)
            m_new = mc if m[c] is None else jnp.maximum(m[c], mc)
            e = jnp.exp2((s - m_new) * c2)
            es = jnp.sum(e, axis=1, keepdims=True)
            pv = jnp.dot(e.astype(BF16), vv, preferred_element_type=F32)
            if m[c] is None:
                l[c], acc[c] = es, pv
            else:
                shrink = jnp.exp2((m[c] - m_new) * c2)
                l[c] = shrink * l[c] + es
                acc[c] = shrink * acc[c] + pv
            m[c] = m_new
    o = acc[0] * (1.0 / l[0]) - acc[1] * (lam / l[1])
    o = o * lax.rsqrt(jnp.mean(o * o, axis=1, keepdims=True) + SUBLN_EPSILON) * sub_ref[...] * (1.0 - lam_init)
    o_ref[...] = o.astype(o_ref.dtype)


def _attn_ctx_call(qkv, lam_p, subln, n_seq, length, d, dh, lam_init):
    nh = d // (2 * dh)
    tq = _divisor(length, 256, HALO)
    per_seq = length // tq
    est = 2 * (tq + 2 * length) * 2 * dh * 4 + 10 * tq * length * 4 + 4 * length * 2 * dh * 2
    return pl.pallas_call(
        functools.partial(_attn_kernel, latent=False, dh=dh, lam_init=lam_init),
        grid=(n_seq, nh, per_seq),
        in_specs=[pl.BlockSpec((4, dh), lambda b, h, i: (0, 0)),
                  pl.BlockSpec((1, 2 * dh), lambda b, h, i: (0, 0)),
                  pl.BlockSpec((tq, 2 * dh), lambda b, h, i: (b * per_seq + i, h)),
                  pl.BlockSpec((length, 2 * dh), lambda b, h, i: (b, nh + h)),
                  pl.BlockSpec((length, 2 * dh), lambda b, h, i: (b, 2 * nh + h))],
        out_specs=pl.BlockSpec((tq, 2 * dh), lambda b, h, i: (b * per_seq + i, h)),
        out_shape=jax.ShapeDtypeStruct((n_seq * length, d), BF16),
        compiler_params=_params(("parallel", "parallel", "arbitrary"), est),
        name="attn_context",
    )(lam_p, subln.reshape(1, 2 * dh), qkv, qkv, qkv)


def _attn_lat_call(qkv, kr, vb, ck, cv, cos, sin, lam_p, subln, tp, n_seq, length, d, dh, lam_init):
    nh = d // (2 * dh)
    lc = ck.shape[1]
    tq = _divisor(math.gcd(length, tp), 256, HALO)
    per_seq = length // tq
    est = (4 * (length + lc) * 2 * dh * 2 + 4 * lc * 2 * dh * 4 + 4 * tq * 2 * dh * 4
           + 5 * tq * (length + lc) * 4)
    return pl.pallas_call(
        functools.partial(_attn_kernel, latent=True, dh=dh, lam_init=lam_init),
        grid=(n_seq, nh, per_seq),
        in_specs=[pl.BlockSpec((4, dh), lambda b, h, i: (0, 0)),
                  pl.BlockSpec((1, 2 * dh), lambda b, h, i: (0, 0)),
                  pl.BlockSpec((tq, 2 * dh), lambda b, h, i: (tp // tq + b * per_seq + i, h)),
                  pl.BlockSpec((length, 2 * dh), lambda b, h, i: (b, h)),
                  pl.BlockSpec((length, 2 * dh), lambda b, h, i: (b, h)),
                  pl.BlockSpec((None, lc, 2 * dh), lambda b, h, i: (b, 0, h)),
                  pl.BlockSpec((None, lc, 2 * dh), lambda b, h, i: (b, 0, h)),
                  pl.BlockSpec((tq, dh), lambda b, h, i: (i, 0)),
                  pl.BlockSpec((tq, dh), lambda b, h, i: (i, 0))],
        out_specs=pl.BlockSpec((tq, 2 * dh), lambda b, h, i: (b * per_seq + i, h)),
        out_shape=jax.ShapeDtypeStruct((n_seq * length, d), BF16),
        compiler_params=_params(("parallel", "parallel", "arbitrary"), est),
        name="attn_latent",
    )(lam_p, subln.reshape(1, 2 * dh), qkv, kr, vb, ck, cv, cos, sin)


def _rope_tables(length, dh):
    half = dh // 2
    pos = jnp.arange(length)
    pos_r = (pos // GRID_COLS).astype(F32)
    pos_c = (pos % GRID_COLS).astype(F32)
    inv = jnp.power(ROPE_THETA, -jnp.arange(0, half, 2, dtype=F32) / half)
    ang_r = pos_r[:, None] * inv[None]
    ang_c = pos_c[:, None] * inv[None]
    cos = jnp.concatenate([jnp.cos(ang_r)] * 2 + [jnp.cos(ang_c)] * 2, axis=1)
    sin = jnp.concatenate([-jnp.sin(ang_r), jnp.sin(ang_r), -jnp.sin(ang_c), jnp.sin(ang_c)], axis=1)
    return cos, sin


def kernel(x_prompt, x_sample, cache_k, cache_v, c, c_ctx, mod_w, mod_b, ln1_g, ln1_b, ln2_g, ln2_b, pool_w, pool_scale, hyena_w_in, hyena_conv, hyena_ffn_w1, hyena_ffn_b1, hyena_ffn_w2, hyena_ffn_b2, hyena_ffn_w3, hyena_ffn_b3, hyena_freq, hyena_decay, hyena_skip, hyena_w_out, attn_w_qkv, attn_lambda, attn_subln, attn_w_out, ffn_w_up, ffn_conv, ffn_w_down):
    n_ctx, s_len, d = x_prompt.shape
    n_lat, ds, _ = x_sample.shape
    depth = mod_w.shape[0]
    tp, ts = n_ctx * s_len, n_lat * ds
    t = tp + ts
    dims = (tp, s_len, ds)
    alpha = (2 * depth) ** 0.25
    nh = cache_k.shape[3]
    dh = cache_k.shape[4] // 2
    lc = cache_k.shape[2]
    assert tp % ds == 0
    assert n_lat + 1 <= 8

    cond8 = jnp.zeros((8, d), F32).at[0].set(c_ctx).at[1:1 + n_lat].set(c)
    mods = _mods_call(cond8, mod_w, mod_b)
    mods4 = mods.reshape(depth, 8, 1, 6 * d)

    x = jnp.concatenate([x_prompt.reshape(tp, d), x_sample.reshape(ts, d)], axis=0)

    pos = jnp.concatenate([jnp.arange(tp) % s_len, jnp.arange(ts) % ds])
    last = jnp.concatenate([jnp.full((tp,), s_len - 1), jnp.full((ts,), ds - 1)])
    edge = jnp.stack([(pos != 0), (pos != last)], axis=1).astype(F32)

    w_down16 = ffn_w_down.astype(BF16)
    f = ffn_w_down.shape[1]
    w_gate16 = ffn_w_up[:, :, :f].astype(BF16)
    w_up16 = ffn_w_up[:, :, f:].astype(BF16)
    conv_gate, conv_up = ffn_conv[:, :, :f], ffn_conv[:, :, f:]
    hyena_in16 = hyena_w_in.astype(BF16)
    qkv16 = attn_w_qkv.astype(BF16)
    hyena_out16 = hyena_w_out.astype(BF16)
    attn_out16 = attn_w_out.astype(BF16)

    new_k = new_v = None
    h_in = None
    for i in range(depth):
        kind, j = i % N_MIXER_KINDS, i // N_MIXER_KINDS
        if kind == 0:
            x, h2 = _pool_call(x, mods4, i, pool_w[j].astype(BF16), pool_scale[j], ln1_g[i], ln1_b[i], dims, alpha)
        elif kind == 1:
            p_even, p_odd = _up_conv_call(h_in, edge, hyena_in16, hyena_conv, j)
            z2s = []
            for length, n_seq, row0 in ((s_len, n_ctx, 0), (ds, n_lat, tp // ds)):
                hk, asum = _filter_call(length, hyena_ffn_w1[j], hyena_ffn_b1[j], hyena_ffn_w2[j], hyena_ffn_b2[j],
                                        hyena_ffn_w3[j], hyena_ffn_b3[j], hyena_freq[j], hyena_decay[j])
                half = length // 2
                tq = _divisor(half, 256, SUBLANES_F32)
                wf, wft = _dft_tables(half, tq)
                phase = jnp.arange(half, dtype=F32) * (math.pi / length)
                tw = jnp.stack([jnp.cos(phase), jnp.sin(phase)], axis=1)
                planes = _kf_call(wf, tw, hk, asum, tq)
                tabs = (wf, wft, tw, tq)
                pe3 = p_even.reshape(t // length, half, 3 * d)
                po3 = p_odd.reshape(t // length, half, 3 * d)
                z1e, z1o = _dftconv_call(tabs, planes, 0, hyena_skip[j, 0:1], (pe3, row0, 0), (po3, row0, 0),
                                         (pe3, row0, 1), (po3, row0, 1), n_seq, d, False)
                z2, = _dftconv_call(tabs, planes, 1, hyena_skip[j, 1:2], (z1e, 0, 0), (z1o, 0, 0),
                                    (pe3, row0, 2), (po3, row0, 2), n_seq, d, True)
                z2s.append(z2.reshape(n_seq * length, d))
            z2 = jnp.concatenate(z2s, axis=0)
            x, h2 = _down_ln_call(z2, hyena_out16, j, x, mods4, i, 2, i, 3, ln1_g[i], ln1_b[i],
                                  dims, alpha, "hyena_out")
        else:
            lam_init = 0.8 - 0.6 * math.exp(-0.3 * i)
            qkv = _up_plain_call(h_in, qkv16, j)
            new_k = qkv[:tp, d:2 * d].reshape(n_ctx, s_len, nh, 2 * dh)
            new_v = qkv[:tp, 2 * d:3 * d].reshape(n_ctx, s_len, nh, 2 * dh)
            o_ctx = _attn_ctx_call(qkv, attn_lambda[j], attn_subln[j], n_ctx, s_len, d, dh, lam_init)
            cos, sin = _rope_tables(ds, dh)
            kr, vb = _rope_kv_call(qkv, cos, sin, tp, d, dh)
            o_lat = _attn_lat_call(qkv, kr, vb, cache_k[:, j].reshape(n_lat, lc, d), cache_v[:, j].reshape(n_lat, lc, d),
                                   cos, sin, attn_lambda[j], attn_subln[j], tp, n_lat, ds, d, dh, lam_init)
            o = jnp.concatenate([o_ctx, o_lat], axis=0)
            x, h2 = _down_ln_call(o, attn_out16, j, x, mods4, i, 2, i, 3, ln1_g[i], ln1_b[i],
                                  dims, alpha, "attn_out")
        a = _up_ffn_call(h2, edge, w_gate16, w_up16, conv_gate, conv_up, i)
        last_layer = i == depth - 1
        x, h_in = _down_ln_call(a, w_down16, i, x, mods4, i, 5, i + 1, None if last_layer else 0,
                                ln2_g[i], ln2_b[i], dims, alpha, "ffn_down")

    y_prompt = x[:tp].reshape(n_ctx, s_len, d)
    y_sample = x[tp:].reshape(n_lat, ds, d)
    return (y_prompt, y_sample, new_k[:, None], new_v[:, None])
```

```python
import functools
import math

import jax
import jax.numpy as jnp
from jax import lax
from jax.experimental import pallas as pl
from jax.experimental.pallas import tpu as pltpu

F32 = jnp.float32
BF16 = jnp.bfloat16

GRID_COLS = 64
POOL_WINDOW_SIZES = (2, 4, 8, 16)
ROPE_THETA = 10000.0
LN_EPSILON = 1e-6
SUBLN_EPSILON = 1e-5
N_MIXER_KINDS = 3

V7X_VMEM_BYTES = 64 * 1024 * 1024
V7X_VMEM_BUDGET = V7X_VMEM_BYTES - 6 * 1024 * 1024
SUBLANES_F32 = 8
SUBLANES_BF16 = 16
LANES = 128
KV_CHUNK = 1024
LN_CHUNK_ROWS = 64


def _params(sem, est_bytes):
    limit = int(min(V7X_VMEM_BUDGET, max(32 * 1024 * 1024, est_bytes * 5 // 4)))
    return pltpu.CompilerParams(dimension_semantics=sem, vmem_limit_bytes=limit)


def _divisor(n, pref, mult):
    best = None
    d = mult
    while d <= min(n, pref):
        if n % d == 0:
            best = d
        d += mult
    assert best is not None, (n, pref, mult)
    return best


def _layer_norm(r, g, b):
    mu = jnp.mean(r, -1, keepdims=True)
    d = r - mu
    var = jnp.mean(d * d, -1, keepdims=True)
    return d * lax.rsqrt(var + LN_EPSILON) * g + b


def _mod_row(t0, tp, ds):
    return jnp.where(t0 < tp, 0, 1 + (t0 - tp) // ds)


def _mods_kernel(c_ref, w_ref, b_ref, o_ref):
    k = pl.program_id(2)

    @pl.when(k == 0)
    def _():
        o_ref[0] = jnp.broadcast_to(b_ref[0], o_ref.shape[1:])

    c = c_ref[...]
    s = (c * jax.nn.sigmoid(c)).astype(BF16)
    o_ref[0] += jnp.dot(s, w_ref[0].astype(BF16), preferred_element_type=F32)


def _mods_call(cond8, mod_w, mod_b):
    depth, d, n = mod_w.shape
    tk = _divisor(d, 2048, LANES)
    tn = _divisor(n, 1024, LANES)
    est = 2 * tk * tn * 4 + 4 * 8 * tn * 4 + 2 * 8 * tk * 4 + tk * tn * 2
    return pl.pallas_call(
        _mods_kernel,
        grid=(depth, n // tn, d // tk),
        in_specs=[pl.BlockSpec((8, tk), lambda l, j, k: (0, k)),
                  pl.BlockSpec((1, tk, tn), lambda l, j, k: (l, k, j)),
                  pl.BlockSpec((1, 1, tn), lambda l, j, k: (l, 0, j))],
        out_specs=pl.BlockSpec((1, 8, tn), lambda l, j, k: (l, 0, j)),
        out_shape=jax.ShapeDtypeStruct((depth, 8, n), F32),
        compiler_params=_params(("parallel", "parallel", "arbitrary"), est),
        name="mods",
    )(cond8, mod_w, mod_b.reshape(depth, 1, n))


def _pool_kernel(xm_ref, xp_ref, xn_ref, mod_ref, pw_ref, ps_ref, lg_ref, lb_ref, xo_ref, ho_ref, e_ref,
                 *, tm, tp, s_len, ds, d, alpha):
    t0 = pl.program_id(0) * tm
    is_ctx = t0 < tp
    seq_len = jnp.where(is_ctx, s_len, ds)
    pos0 = jnp.where(is_ctx, t0 % s_len, (t0 - tp) % ds)
    mods = mod_ref[...]
    sh1, sc1, g1 = mods[:, 0:d], mods[:, d:2 * d], mods[:, 2 * d:3 * d]
    sh2, sc2 = mods[:, 3 * d:4 * d], mods[:, 4 * d:5 * d]
    x = xm_ref[...]
    h = x * (1.0 + sc1) + sh1
    has_prev = pos0 > 0
    has_next = pos0 + tm < seq_len
    e_ref[0:8, :] = jnp.where(has_prev, xp_ref[...] * (1.0 + sc1) + sh1, 0.0)
    e_ref[8:8 + tm, :] = h
    e_ref[8 + tm:16 + tm, :] = jnp.where(has_next, xn_ref[...] * (1.0 + sc1) + sh1, 0.0)
    t = pos0 + lax.broadcasted_iota(jnp.int32, (tm, 1), 0)
    cg = d // len(POOL_WINDOW_SIZES)
    ys = []
    for g, w in enumerate(POOL_WINDOW_SIZES):
        c0 = g * cg
        acc = None
        for s in range(-(w // 2), w - w // 2):
            v = e_ref[8 + s:8 + s + tm, c0:c0 + cg]
            acc = v if acc is None else acc + v
        lo = jnp.maximum(t - w // 2, 0)
        hi = jnp.minimum(t + (w - w // 2), seq_len)
        inv = 1.0 / (hi - lo).astype(F32)
        p = acc * inv - h[:, c0:c0 + cg]
        ys.append(jnp.dot(p.astype(BF16), pw_ref[g], preferred_element_type=F32))
    y = jnp.concatenate(ys, axis=1) * ps_ref[...]
    xn = _layer_norm(alpha * x + g1 * y, lg_ref[...], lb_ref[...])
    xo_ref[...] = xn
    ho_ref[...] = (xn * (1.0 + sc2) + sh2).astype(BF16)


def _pool_call(x, mods4, layer, pw, ps, lg, lb, dims, alpha):
    t, d = x.shape
    tp, s_len, ds = dims
    tm = s_len
    assert ds % tm == 0 and tm % SUBLANES_BF16 == 0
    n8 = t // 8
    kern = functools.partial(_pool_kernel, tm=tm, tp=tp, s_len=s_len, ds=ds, d=d, alpha=alpha)
    g, cg, _ = pw.shape
    est = (2 * tm * d * 4 * 2 + 2 * tm * d * 2 + (tm + 16) * d * 4 + 2 * g * cg * cg * 2
           + 6 * tm * d * 4)
    return pl.pallas_call(
        kern,
        grid=(t // tm,),
        in_specs=[pl.BlockSpec((tm, d), lambda i: (i, 0)),
                  pl.BlockSpec((8, d), lambda i: (jnp.maximum(i * (tm // 8) - 1, 0), 0)),
                  pl.BlockSpec((8, d), lambda i: (jnp.minimum((i + 1) * (tm // 8), n8 - 1), 0)),
                  pl.BlockSpec((None, None, 1, 6 * d), lambda i: (layer, _mod_row(i * tm, tp, ds), 0, 0)),
                  pl.BlockSpec((g, cg, cg), lambda i: (0, 0, 0)),
                  pl.BlockSpec((1, d), lambda i: (0, 0)),
                  pl.BlockSpec((1, d), lambda i: (0, 0)),
                  pl.BlockSpec((1, d), lambda i: (0, 0))],
        out_specs=[pl.BlockSpec((tm, d), lambda i: (i, 0)),
                   pl.BlockSpec((tm, d), lambda i: (i, 0))],
        out_shape=[jax.ShapeDtypeStruct((t, d), F32), jax.ShapeDtypeStruct((t, d), BF16)],
        scratch_shapes=[pltpu.VMEM((tm + 16, d), F32)],
        compiler_params=_params(("parallel",), est),
        name="pool_mixer",
    )(x, x, x, mods4, pw, ps.reshape(1, d), lg.reshape(1, d), lb.reshape(1, d))


HALO = SUBLANES_BF16


def _fill_ext(ext_ref, a_ref, ap_ref, an_ref, tm):
    @pl.when(pl.program_id(1) == 0)
    def _():
        ext_ref[0:HALO, :] = ap_ref[...]
        ext_ref[HALO:HALO + tm, :] = a_ref[...]
        ext_ref[HALO + tm:2 * HALO + tm, :] = an_ref[...]


def _conv3(p, cw, not_first, not_last, tm):
    rows = p.shape[0]
    up = pltpu.roll(p, 1, 0)[HALO:HALO + tm]
    dn = pltpu.roll(p, rows - 1, 0)[HALO:HALO + tm]
    mid = p[HALO:HALO + tm]
    up = jnp.where(not_first > 0.0, up, 0.0)
    dn = jnp.where(not_last > 0.0, dn, 0.0)
    return cw[0:1] * up + cw[1:2] * mid + cw[2:3] * dn


def _up_ffn_kernel(a_ref, ap_ref, an_ref, edge_ref, wg_ref, wu0_ref, wu1_ref, cg_ref, cu0_ref, cu1_ref, o_ref,
                   ext_ref, *, tm):
    _fill_ext(ext_ref, a_ref, ap_ref, an_ref, tm)
    e = edge_ref[...]
    nf, nl = e[:, 0:1], e[:, 1:2]
    a = ext_ref[...]
    g = _conv3(jnp.dot(a, wg_ref[...], preferred_element_type=F32), cg_ref[...], nf, nl, tm)
    pu = jnp.concatenate([jnp.dot(a, wu0_ref[...], preferred_element_type=F32),
                          jnp.dot(a, wu1_ref[...], preferred_element_type=F32)], axis=1)
    cu = jnp.concatenate([cu0_ref[...], cu1_ref[...]], axis=1)
    u = _conv3(pu, cu, nf, nl, tm)
    o_ref[...] = (g * jax.nn.sigmoid(g) * u).astype(o_ref.dtype)


def _up_conv_kernel(a_ref, ap_ref, an_ref, edge_ref, w_ref, cw_ref, oe_ref, oo_ref, ext_ref, par_ref, *, tm):
    _fill_ext(ext_ref, a_ref, ap_ref, an_ref, tm)
    e = edge_ref[...]
    p = jnp.dot(ext_ref[...], w_ref[...], preferred_element_type=F32)
    c = _conv3(p, cw_ref[...], e[:, 0:1], e[:, 1:2], tm)
    for k in range(c.shape[1] // LANES):
        cols = slice(k * LANES, (k + 1) * LANES)
        par_ref[k] = c[:, cols]
        oe_ref[:, cols] = par_ref[k, pl.ds(0, tm // 2, stride=2), :]
        oo_ref[:, cols] = par_ref[k, pl.ds(1, tm // 2, stride=2), :]


def _up_plain_kernel(a_ref, w_ref, o_ref):
    o_ref[...] = jnp.dot(a_ref[...], w_ref[...], preferred_element_type=F32).astype(o_ref.dtype)


def _halo_specs(tm, d, t):
    nh = t // HALO
    return [pl.BlockSpec((tm, d), lambda i, j: (i, 0)),
            pl.BlockSpec((HALO, d), lambda i, j: (jnp.maximum(i * (tm // HALO) - 1, 0), 0)),
            pl.BlockSpec((HALO, d), lambda i, j: (jnp.minimum((i + 1) * (tm // HALO), nh - 1), 0)),
            pl.BlockSpec((tm, 2), lambda i, j: (i, 0))]


def _up_ffn_call(h, edge, w_up, conv_w, layer):
    t, d = h.shape
    f = w_up.shape[2] // 2
    tm = _divisor(t, 1024, HALO)
    tn = min(512, f)
    hn = tn // 2
    assert f % hn == 0
    last = 2 * f // hn - 1
    up0 = lambda j: f // hn + 2 * j
    up1 = lambda j: jnp.minimum(f // hn + 2 * j + 1, last)
    est = (2 * tm * d * 2 + (tm + 2 * HALO) * d * 2 + 4 * d * tn * 2 + 2 * tm * tn * 2
           + 8 * (tm + 2 * HALO) * tn * 4)
    return pl.pallas_call(
        functools.partial(_up_ffn_kernel, tm=tm),
        grid=(t // tm, pl.cdiv(f, tn)),
        in_specs=_halo_specs(tm, d, t) + [
            pl.BlockSpec((None, d, tn), lambda i, j: (layer, 0, j)),
            pl.BlockSpec((None, d, hn), lambda i, j: (layer, 0, up0(j))),
            pl.BlockSpec((None, d, hn), lambda i, j: (layer, 0, up1(j))),
            pl.BlockSpec((None, 3, tn), lambda i, j: (layer, 0, j)),
            pl.BlockSpec((None, 3, hn), lambda i, j: (layer, 0, up0(j))),
            pl.BlockSpec((None, 3, hn), lambda i, j: (layer, 0, up1(j)))],
        out_specs=pl.BlockSpec((tm, tn), lambda i, j: (i, j)),
        out_shape=jax.ShapeDtypeStruct((t, f), BF16),
        scratch_shapes=[pltpu.VMEM((tm + 2 * HALO, d), BF16)],
        compiler_params=_params(("parallel", "arbitrary"), est),
        name="ffn_up",
    )(h, h, h, edge, w_up, w_up, w_up, conv_w, conv_w, conv_w)


def _up_conv_call(h, edge, w, cw, layer):
    t, d = h.shape
    n = w.shape[2]
    tm = _divisor(t, 1024, HALO)
    tn = _divisor(n, 512, LANES)
    est = (2 * tm * d * 2 + (tm + 2 * HALO) * d * 2 + 2 * d * tn * 2 + 2 * tm * tn * 4
           + 5 * (tm + 2 * HALO) * tn * 4)
    return pl.pallas_call(
        functools.partial(_up_conv_kernel, tm=tm),
        grid=(t // tm, n // tn),
        in_specs=_halo_specs(tm, d, t) + [
            pl.BlockSpec((None, d, tn), lambda i, j: (layer, 0, j)),
            pl.BlockSpec((None, 3, tn), lambda i, j: (layer, 0, j))],
        out_specs=[pl.BlockSpec((tm // 2, tn), lambda i, j: (i, j))] * 2,
        out_shape=[jax.ShapeDtypeStruct((t // 2, n), F32)] * 2,
        scratch_shapes=[pltpu.VMEM((tm + 2 * HALO, d), BF16), pltpu.VMEM((tn // LANES, tm, LANES), F32)],
        compiler_params=_params(("parallel", "arbitrary"), est),
        name="hyena_in",
    )(h, h, h, edge, w, cw)


def _up_plain_call(h, w, layer):
    t, d = h.shape
    n = w.shape[2]
    tm = _divisor(t, 1024, HALO)
    tn = _divisor(n, 512, LANES)
    est = 2 * tm * d * 2 + 2 * d * tn * 2 + 2 * tm * tn * 4 + 2 * tm * tn * 4
    return pl.pallas_call(
        _up_plain_kernel,
        grid=(t // tm, n // tn),
        in_specs=[pl.BlockSpec((tm, d), lambda i, j: (i, 0)),
                  pl.BlockSpec((None, d, tn), lambda i, j: (layer, 0, j))],
        out_specs=pl.BlockSpec((tm, tn), lambda i, j: (i, j)),
        out_shape=jax.ShapeDtypeStruct((t, n), F32),
        compiler_params=_params(("parallel", "arbitrary"), est),
        name="attn_qkv",
    )(h, w)


def _down_ln_kernel(*refs, nj, tn, d, gate, alpha, nxt):
    if nxt is None:
        a_ref, w_ref, x_ref, mod_ref, lg_ref, lb_ref, xo_ref = refs
    else:
        a_ref, w_ref, x_ref, mod_ref, modn_ref, lg_ref, lb_ref, xo_ref, ho_ref = refs
    j = pl.program_id(1)
    col = pl.multiple_of(j * tn, tn)
    xo_ref[:, pl.ds(col, tn)] = jnp.dot(a_ref[...], w_ref[...], preferred_element_type=F32)

    @pl.when(j == nj - 1)
    def _():
        g = mod_ref[:, gate * d:(gate + 1) * d]
        tm = xo_ref.shape[0]
        ch = math.gcd(tm, LN_CHUNK_ROWS)

        def chunk(r, carry):
            rows = pl.ds(pl.multiple_of(r * ch, ch), ch)
            xn = _layer_norm(alpha * x_ref[rows, :] + g * xo_ref[rows, :], lg_ref[...], lb_ref[...])
            xo_ref[rows, :] = xn
            if nxt is not None:
                sh = modn_ref[:, nxt * d:(nxt + 1) * d]
                sc = modn_ref[:, (nxt + 1) * d:(nxt + 2) * d]
                ho_ref[rows, :] = (xn * (1.0 + sc) + sh).astype(BF16)
            return carry

        lax.fori_loop(0, tm // ch, chunk, 0)


def _down_ln_call(a, w, wlayer, x, mods4, layer, gate, nxt_layer, nxt, lg, lb, dims, alpha, name):
    t, kdim = a.shape
    d = w.shape[2]
    tp, s_len, ds = dims
    tm = _divisor(math.gcd(tp, ds), 512, HALO)
    tn = _divisor(d, 512, LANES)
    nj = d // tn
    kern = functools.partial(_down_ln_kernel, nj=nj, tn=tn, d=d, gate=gate, alpha=alpha, nxt=nxt)
    fixed = 2 * kdim * tn * 2 + tm * d * 4 + 2 * tm * tn * 4 + (4 << 20)
    per_tile = tm * kdim * 2 + tm * d * 4 + (tm * d * 2 if nxt is not None else 0)
    once = pl.Buffered(1)
    rest = pl.Buffered(2) if fixed + 2 * per_tile <= V7X_VMEM_BUDGET else once
    mod_spec = lambda l: pl.BlockSpec((None, None, 1, 6 * d),
                                      lambda i, k: (l, _mod_row(i * tm, tp, ds), 0, 0))
    in_specs = [pl.BlockSpec((tm, kdim), lambda i, k: (i, 0), pipeline_mode=rest),
                pl.BlockSpec((None, kdim, tn), lambda i, k: (wlayer, 0, k)),
                pl.BlockSpec((tm, d), lambda i, k: (i, 0), pipeline_mode=once),
                mod_spec(layer)]
    args = [a, w, x, mods4]
    if nxt is not None:
        in_specs.append(mod_spec(nxt_layer))
        args.append(mods4)
    in_specs += [pl.BlockSpec((1, d), lambda i, k: (0, 0)), pl.BlockSpec((1, d), lambda i, k: (0, 0))]
    args += [lg.reshape(1, d), lb.reshape(1, d)]
    out_specs = [pl.BlockSpec((tm, d), lambda i, k: (i, 0), pipeline_mode=rest)]
    out_shape = [jax.ShapeDtypeStruct((t, d), F32)]
    if nxt is not None:
        out_specs.append(pl.BlockSpec((tm, d), lambda i, k: (i, 0), pipeline_mode=rest))
        out_shape.append(jax.ShapeDtypeStruct((t, d), BF16))
    est = fixed + (2 if rest is not once else 1) * per_tile
    res = pl.pallas_call(
        kern,
        grid=(t // tm, nj),
        in_specs=in_specs,
        out_specs=out_specs,
        out_shape=out_shape,
        compiler_params=_params(("parallel", "arbitrary"), est),
        name=name,
    )(*args)
    return (res[0], res[1]) if nxt is not None else (res[0], None)


def _dot_f32(a, b):
    return lax.dot_general(a, b, (((1,), (0,)), ((), ())), precision=lax.Precision.HIGHEST,
                           preferred_element_type=F32)


def _filter_kernel(z_ref, tu_ref, w1_ref, b1_ref, w2_ref, b2_ref, w3_ref, b3_ref, fr_ref, dec_ref,
                   hk_ref, as_ref, *, tl, n_col_tiles):
    jc = pl.program_id(0)
    i = pl.program_id(1)
    h = jnp.sin(fr_ref[0:1, :] * (_dot_f32(z_ref[...], w1_ref[...]) + b1_ref[...]))
    h = jnp.sin(fr_ref[1:2, :] * (_dot_f32(h, w2_ref[...]) + b2_ref[...]))
    h = jnp.dot(h.astype(BF16), w3_ref[...].astype(BF16), preferred_element_type=F32) + b3_ref[...]
    h = h * jnp.exp(-tu_ref[...] * jnp.abs(dec_ref[...]))
    row = i * tl + lax.broadcasted_iota(jnp.int32, (tl, 1), 0)
    drop = jnp.logical_and(row == 0, jc >= n_col_tiles // 2)
    h = jnp.where(drop, 0.0, h)
    hk_ref[...] = h.astype(BF16)
    part = jnp.sum(jnp.abs(h), axis=0, keepdims=True)

    @pl.when(i == 0)
    def _():
        as_ref[...] = part

    @pl.when(i > 0)
    def _():
        as_ref[...] += part


def _filter_call(length, w1, b1, w2, b2, w3, b3, freq, decay):
    emb, width = w1.shape
    n = w3.shape[1]
    t = jnp.arange(length).reshape(length // 2, 2).T.reshape(length).astype(F32)
    t_unit = t / max(length - 1, 1)
    n_bands = (emb - 1) // 2
    bands = jnp.linspace(1e-4, n_bands - 1, n_bands, dtype=F32)
    ang = (2.0 * math.pi / length) * t[:, None] * bands[None, :]
    z = jnp.concatenate([t_unit[:, None], jnp.cos(ang), -jnp.sin(ang)], -1)
    z = jnp.pad(z, ((0, 0), (0, LANES - emb)))
    w1p = jnp.pad(w1, ((0, LANES - emb), (0, 0)))
    tl = _divisor(length, 256, 8)
    tn = _divisor(n // 2, 2048, LANES)
    nct = n // tn
    est = 2 * tl * tn * 2 + 6 * tl * tn * 4 + 4 * width * tn * 4
    return pl.pallas_call(
        functools.partial(_filter_kernel, tl=tl, n_col_tiles=nct),
        grid=(nct, length // tl),
        in_specs=[pl.BlockSpec((tl, LANES), lambda j, i: (i, 0)),
                  pl.BlockSpec((tl, 1), lambda j, i: (i, 0)),
                  pl.BlockSpec((LANES, width), lambda j, i: (0, 0)),
                  pl.BlockSpec((1, width), lambda j, i: (0, 0)),
                  pl.BlockSpec((width, width), lambda j, i: (0, 0)),
                  pl.BlockSpec((1, width), lambda j, i: (0, 0)),
                  pl.BlockSpec((width, tn), lambda j, i: (0, j)),
                  pl.BlockSpec((1, tn), lambda j, i: (0, j)),
                  pl.BlockSpec((2, width), lambda j, i: (0, 0)),
                  pl.BlockSpec((1, tn), lambda j, i: (0, j))],
        out_specs=[pl.BlockSpec((tl, tn), lambda j, i: (i, j)),
                   pl.BlockSpec((1, tn), lambda j, i: (0, j))],
        out_shape=[jax.ShapeDtypeStruct((length, n), BF16), jax.ShapeDtypeStruct((1, n), F32)],
        compiler_params=_params(("parallel", "arbitrary"), est),
        name="hyena_filter",
    )(z, t_unit[:, None], w1p, b1.reshape(1, width), w2, b2.reshape(1, width), w3, b3.reshape(1, n),
      freq, decay.reshape(1, n))


def _dft_tables(length, tq):
    n2 = 2 * length
    nq = length // tq
    n = jnp.arange(length, dtype=jnp.int32)
    pa = (jnp.arange(nq, dtype=jnp.int32)[:, None] * tq * n[None, :]) % n2
    pb = (jnp.arange(tq, dtype=jnp.int32)[:, None] * n[None, :]) % n2
    a = pa.astype(F32) * (2.0 * math.pi / n2)
    b = pb.astype(F32) * (2.0 * math.pi / n2)
    ca, sa = jnp.cos(a)[:, None, :], jnp.sin(a)[:, None, :]
    cb, sb = jnp.cos(b)[None, :, :], jnp.sin(b)[None, :, :]
    cos_rows = ca * cb - sa * sb
    sin_rows = -(sa * cb + ca * sb)
    nyq = jnp.where(n % 2 == 0, 1.0, -1.0)[None, None, :]
    first = jnp.logical_and(lax.broadcasted_iota(jnp.int32, (nq, tq, 1), 0) == 0,
                            lax.broadcasted_iota(jnp.int32, (nq, tq, 1), 1) == 0)
    sin_rows = jnp.where(first, nyq, sin_rows)
    wf = jnp.concatenate([cos_rows, sin_rows], axis=1).reshape(2 * length, length).astype(BF16)
    return wf, wf.T


def _fold_fwd(e, o, c, s, f0, tq):
    ere, eim, ore, oim = e[:tq], e[tq:], o[:tq], o[tq:]
    tre = ore * c + oim * s
    tim = oim * c - ore * s
    pre = ere + tre
    pim = jnp.where(f0, ere - ore, eim + tim)
    qre = jnp.where(f0, eim, ere - tre)
    qim = jnp.where(f0, -oim, tim - eim)
    return pre, pim, qre, qim


def _kf_kernel(wf_ref, fe_ref, fo_ref, be_ref, bo_ref, af_ref, ab_ref, tw_ref,
               kpre_ref, kpim_ref, kqre_ref, kqim_ref, *, tq, n2):
    q = pl.program_id(1)
    w = wf_ref[...]
    dot = lambda r: jnp.dot(w, r[...], preferred_element_type=F32)
    f0 = (q * tq + lax.broadcasted_iota(jnp.int32, (tq, 1), 0)) == 0
    c, s = tw_ref[:, 0:1], tw_ref[:, 1:2]
    fpre, fpim, fqre, fqim = _fold_fwd(dot(fe_ref), dot(fo_ref), c, s, f0, tq)
    bpre, bpim, bqre, bqim = _fold_fwd(dot(be_ref), dot(bo_ref), c, s, f0, tq)
    inv = 1.0 / (af_ref[...] + ab_ref[...] + 1e-6)
    wq = (2.0 / n2) * inv
    wp = jnp.where(f0, 1.0 / n2, 2.0 / n2) * inv
    kpre_ref[...] = (fpre + bpre) * wp
    kpim_ref[...] = jnp.where(f0, fpim + bpim, fpim - bpim) * wp
    kqre_ref[...] = (fqre + bqre) * wq
    kqim_ref[...] = (fqim - bqim) * wq


def _kf_call(wf, tw, hk, asum, tq):
    half = hk.shape[0] // 2
    n = hk.shape[1] // 2
    dt = _divisor(n, 512, LANES)
    nq = half // tq
    nb = n // dt
    est = 2 * 2 * tq * half * 2 + 8 * half * dt * 2 + 8 * tq * dt * 4 + 24 * tq * dt * 4
    plane = jax.ShapeDtypeStruct((half, n), F32)
    taps = lambda parity, col0: pl.BlockSpec((half, dt), lambda j, q: (parity, col0 + j))
    return pl.pallas_call(
        functools.partial(_kf_kernel, tq=tq, n2=4 * half),
        grid=(nb, nq),
        in_specs=[pl.BlockSpec((2 * tq, half), lambda j, q: (q, 0)),
                  taps(0, 0), taps(1, 0), taps(0, nb), taps(1, nb),
                  pl.BlockSpec((1, dt), lambda j, q: (0, j)),
                  pl.BlockSpec((1, dt), lambda j, q: (0, nb + j)),
                  pl.BlockSpec((tq, 2), lambda j, q: (q, 0))],
        out_specs=[pl.BlockSpec((tq, dt), lambda j, q: (q, j))] * 4,
        out_shape=[plane] * 4,
        compiler_params=_params(("parallel", "arbitrary"), est),
        name="hyena_filter_dft",
    )(wf, hk, hk, hk, hk, asum, asum, tw)


def _dftconv_kernel(*refs, nb, nq, tq, dt, interleave):
    wf_ref, wft_ref, tw_ref, kpre_ref, kpim_ref, kqre_ref, kqim_ref, skip_ref = refs[:8]
    seq = lambda k: refs[8 + k * nb:8 + (k + 1) * nb]
    ue_in, uo_in, ge_in, go_in = seq(0), seq(1), seq(2), seq(3)
    n_out = 1 if interleave else 2
    outs = refs[8 + 4 * nb:8 + 4 * nb + n_out]
    ue_ref, uo_ref, ae_ref, ao_ref = refs[8 + 4 * nb + n_out:12 + 4 * nb + n_out]
    q = pl.program_id(2)
    half = wft_ref.shape[0]

    @pl.when(q == 0)
    def _():
        for b in range(nb):
            cols = slice(b * dt, (b + 1) * dt)
            ue = ue_in[b][...]
            uo = uo_in[b][...]
            ue_ref[:, cols] = ue.astype(BF16)
            uo_ref[:, cols] = uo.astype(BF16)
            ae_ref[:, cols] = ue * skip_ref[...]
            ao_ref[:, cols] = uo * skip_ref[...]

    w = wf_ref[...]
    f0 = (q * tq + lax.broadcasted_iota(jnp.int32, (tq, 1), 0)) == 0
    c, s = tw_ref[:, 0:1], tw_ref[:, 1:2]
    pre, pim, qre, qim = _fold_fwd(jnp.dot(w, ue_ref[...], preferred_element_type=F32),
                                   jnp.dot(w, uo_ref[...], preferred_element_type=F32), c, s, f0, tq)
    wide = lambda r: jnp.concatenate([r[...]] * nb, axis=1)
    kpre, kpim, kqre, kqim = wide(kpre_ref), wide(kpim_ref), wide(kqre_ref), wide(kqim_ref)
    ypre = pre * kpre - jnp.where(f0, 0.0, pim * kpim)
    ypim = jnp.where(f0, pim * kpim, pre * kpim + pim * kpre)
    yqre = qre * kqre - qim * kqim
    yqim = qre * kqim + qim * kqre
    dre, dim = ypre - yqre, ypim + yqim
    z0re = jnp.where(f0, ypre + ypim, ypre + yqre)
    z0im = jnp.where(f0, yqre, ypim - yqim)
    z1re = jnp.where(f0, ypre - ypim, dre * c - dim * s)
    z1im = jnp.where(f0, -yqim, dre * s + dim * c)
    wt = wft_ref[...]
    ae_ref[...] += jnp.dot(wt, jnp.concatenate([z0re, z0im], axis=0).astype(BF16), preferred_element_type=F32)
    ao_ref[...] += jnp.dot(wt, jnp.concatenate([z1re, z1im], axis=0).astype(BF16), preferred_element_type=F32)

    @pl.when(q == nq - 1)
    def _():
        for b in range(nb):
            cols = slice(b * dt, (b + 1) * dt)
            ye = ge_in[b][...] * ae_ref[:, cols]
            yo = go_in[b][...] * ao_ref[:, cols]
            if not interleave:
                outs[0][b] = ye
                outs[1][b] = yo
                continue
            mix_ref = refs[12 + 4 * nb + n_out]
            for k in range(dt // LANES):
                lanes = slice(k * LANES, (k + 1) * LANES)
                mix_ref[pl.ds(0, half, stride=2), :] = ye[:, lanes]
                mix_ref[pl.ds(1, half, stride=2), :] = yo[:, lanes]
                outs[0][b, :, lanes] = mix_ref[...].astype(outs[0].dtype)


def _dftconv_call(tabs, planes, korder, skip, u_even, u_odd, g_even, g_odd, n_seq, d, interleave):
    wf, wft, tw, tq = tabs
    half = wft.shape[0]
    nq = half // tq
    long_seq = half > 512
    dt = _divisor(d, 256 if long_seq else 512, LANES)
    nb = _divisor(n_seq, 2 if long_seq else 4, 1)
    nd = d // dt
    once = pl.Buffered(1)
    blk = half * dt
    est = (4 * 2 * tq * half * 2 + 6 * nb * blk * 4 + 2 * nb * blk * 2 + 2 * nb * blk * 4
           + 2 * nb * blk * 4 + 40 * tq * nb * dt * 4)
    kern = functools.partial(_dftconv_kernel, nb=nb, nq=nq, tq=tq, dt=dt, interleave=interleave)
    sources = (u_even, u_odd, g_even, g_odd)
    scratch = [pltpu.VMEM((half, nb * dt), BF16), pltpu.VMEM((half, nb * dt), BF16),
               pltpu.VMEM((half, nb * dt), F32), pltpu.VMEM((half, nb * dt), F32)]
    if interleave:
        out_specs = [pl.BlockSpec((nb, 2 * half, dt), lambda s, j, q: (s, 0, j), pipeline_mode=once)]
        out_shape = [jax.ShapeDtypeStruct((n_seq, 2 * half, d), BF16)]
        scratch.append(pltpu.VMEM((2 * half, LANES), F32))
    else:
        out_specs = [pl.BlockSpec((nb, half, dt), lambda s, j, q: (s, 0, j), pipeline_mode=once)] * 2
        out_shape = [jax.ShapeDtypeStruct((n_seq, half, d), F32)] * 2
    seq_specs = lambda row0, col: [
        pl.BlockSpec((None, half, dt), lambda s, j, q, b=b: (row0 + s * nb + b, 0, col * nd + j),
                     pipeline_mode=once) for b in range(nb)]
    plane_spec = pl.BlockSpec((tq, dt), lambda s, j, q: (q, korder * nd + j))
    return pl.pallas_call(
        kern,
        grid=(n_seq // nb, nd, nq),
        in_specs=[pl.BlockSpec((2 * tq, half), lambda s, j, q: (q, 0)),
                  pl.BlockSpec((half, 2 * tq), lambda s, j, q: (0, q)),
                  pl.BlockSpec((tq, 2), lambda s, j, q: (q, 0)),
                  plane_spec, plane_spec, plane_spec, plane_spec,
                  pl.BlockSpec((1, dt), lambda s, j, q: (0, j))]
                 + [spec for _, row0, col in sources for spec in seq_specs(row0, col)],
        out_specs=out_specs,
        out_shape=out_shape,
        scratch_shapes=scratch,
        compiler_params=_params(("parallel", "parallel", "arbitrary"), est),
        name="hyena_conv",
    )(wf, wft, tw, *planes, skip, *[arr for arr, _, _ in sources for _ in range(nb)])


def _rope(x, cos, sin, dh):
    outs = []
    for c in range(2):
        xc = x[:, c * dh:(c + 1) * dh]
        lane = lax.broadcasted_iota(jnp.int32, xc.shape, 1)
        first_half = (lane & 32) == 0
        sw = jnp.where(first_half, pltpu.roll(xc, dh - 32, 1), pltpu.roll(xc, 32, 1))
        outs.append(xc * cos + sw * sin)
    return jnp.concatenate(outs, axis=1)


def _rope_kv_kernel(k_ref, v_ref, cos_ref, sin_ref, ko_ref, vo_ref, *, dh):
    ko_ref[...] = _rope(k_ref[...], cos_ref[...], sin_ref[...], dh).astype(BF16)
    vo_ref[...] = v_ref[...].astype(BF16)


def _rope_kv_call(qkv, cos, sin, tp, d, dh):
    t = qkv.shape[0]
    ts = t - tp
    length = cos.shape[0]
    tr = _divisor(math.gcd(length, tp), 512, HALO)
    nh = d // (2 * dh)
    per_seq = length // tr
    est = 8 * tr * 2 * dh * 4 + 4 * tr * dh * 4
    return pl.pallas_call(
        functools.partial(_rope_kv_kernel, dh=dh),
        grid=(ts // tr, nh),
        in_specs=[pl.BlockSpec((tr, 2 * dh), lambda i, h: (tp // tr + i, nh + h)),
                  pl.BlockSpec((tr, 2 * dh), lambda i, h: (tp // tr + i, 2 * nh + h)),
                  pl.BlockSpec((tr, dh), lambda i, h: (i % per_seq, 0)),
                  pl.BlockSpec((tr, dh), lambda i, h: (i % per_seq, 0))],
        out_specs=[pl.BlockSpec((tr, 2 * dh), lambda i, h: (i, h)),
                   pl.BlockSpec((tr, 2 * dh), lambda i, h: (i, h))],
        out_shape=[jax.ShapeDtypeStruct((ts, d), BF16), jax.ShapeDtypeStruct((ts, d), BF16)],
        compiler_params=_params(("parallel", "parallel"), est),
        name="attn_rope_kv",
    )(qkv, qkv, cos, sin)


def _attn_kernel(*refs, latent, dh, lam_init):
    if latent:
        lam_ref, sub_ref, q_ref, k_ref, v_ref, ck_ref, cv_ref, cos_ref, sin_ref, o_ref = refs
    else:
        lam_ref, sub_ref, q_ref, k_ref, v_ref, o_ref = refs
    scale = dh ** -0.5
    lp = lam_ref[...]
    lam = (jnp.exp(jnp.sum(lp[0:1] * lp[1:2], axis=1, keepdims=True))
           - jnp.exp(jnp.sum(lp[2:3] * lp[3:4], axis=1, keepdims=True)) + lam_init)
    q = q_ref[...]
    if latent:
        q = _rope(q, cos_ref[...], sin_ref[...], dh)
    qb = q.astype(BF16)
    segs = [(k_ref, v_ref)] + ([(ck_ref, cv_ref)] if latent else [])
    chunks = []
    for kr, vr in segs:
        n = kr.shape[0]
        ck = math.gcd(n, KV_CHUNK)
        chunks += [(kr, vr, i * ck, ck) for i in range(n // ck)]
    nt = (((1,), (1,)), ((), ()))
    c2 = scale * math.log2(math.e)
    m = [None, None]
    l = [None, None]
    acc = [None, None]
    for kr, vr, r0, ck in chunks:
        vv = vr[r0:r0 + ck, :].astype(BF16)
        for c in range(2):
            kk = kr[r0:r0 + ck, c * dh:(c + 1) * dh].astype(BF16)
            s = lax.dot_general(qb[:, c * dh:(c + 1) * dh], kk, nt, preferred_element_type=F32)
            mc = jnp.max(s, axis=1, keepdims=True)
            m_new = mc if m[c] is None else jnp.maximum(m[c], mc)
            e = jnp.exp2((s - m_new) * c2)
            es = jnp.sum(e, axis=1, keepdims=True)
            pv = jnp.dot(e.astype(BF16), vv, preferred_element_type=F32)
            if m[c] is None:
                l[c], acc[c] = es, pv
            else:
                shrink = jnp.exp2((m[c] - m_new) * c2)
                l[c] = shrink * l[c] + es
                acc[c] = shrink * acc[c] + pv
            m[c] = m_new
    o = acc[0] * (1.0 / l[0]) - acc[1] * (lam / l[1])
    o = o * lax.rsqrt(jnp.mean(o * o, axis=1, keepdims=True) + SUBLN_EPSILON) * sub_ref[...] * (1.0 - lam_init)
    o_ref[...] = o.astype(o_ref.dtype)


def _attn_ctx_call(qkv, lam_p, subln, n_seq, length, d, dh, lam_init):
    nh = d // (2 * dh)
    tq = _divisor(length, 256, HALO)
    per_seq = length // tq
    est = 2 * (tq + 2 * length) * 2 * dh * 4 + 10 * tq * length * 4 + 4 * length * 2 * dh * 2
    return pl.pallas_call(
        functools.partial(_attn_kernel, latent=False, dh=dh, lam_init=lam_init),
        grid=(n_seq, nh, per_seq),
        in_specs=[pl.BlockSpec((4, dh), lambda b, h, i: (0, 0)),
                  pl.BlockSpec((1, 2 * dh), lambda b, h, i: (0, 0)),
                  pl.BlockSpec((tq, 2 * dh), lambda b, h, i: (b * per_seq + i, h)),
                  pl.BlockSpec((length, 2 * dh), lambda b, h, i: (b, nh + h)),
                  pl.BlockSpec((length, 2 * dh), lambda b, h, i: (b, 2 * nh + h))],
        out_specs=pl.BlockSpec((tq, 2 * dh), lambda b, h, i: (b * per_seq + i, h)),
        out_shape=jax.ShapeDtypeStruct((n_seq * length, d), BF16),
        compiler_params=_params(("parallel", "parallel", "arbitrary"), est),
        name="attn_context",
    )(lam_p, subln.reshape(1, 2 * dh), qkv, qkv, qkv)


def _attn_lat_call(qkv, kr, vb, ck, cv, cos, sin, lam_p, subln, tp, n_seq, length, d, dh, lam_init):
    nh = d // (2 * dh)
    lc = ck.shape[1]
    tq = _divisor(math.gcd(length, tp), 256, HALO)
    per_seq = length // tq
    est = (4 * (length + lc) * 2 * dh * 2 + 4 * lc * 2 * dh * 4 + 4 * tq * 2 * dh * 4
           + 5 * tq * (length + lc) * 4)
    return pl.pallas_call(
        functools.partial(_attn_kernel, latent=True, dh=dh, lam_init=lam_init),
        grid=(n_seq, nh, per_seq),
        in_specs=[pl.BlockSpec((4, dh), lambda b, h, i: (0, 0)),
                  pl.BlockSpec((1, 2 * dh), lambda b, h, i: (0, 0)),
                  pl.BlockSpec((tq, 2 * dh), lambda b, h, i: (tp // tq + b * per_seq + i, h)),
                  pl.BlockSpec((length, 2 * dh), lambda b, h, i: (b, h)),
                  pl.BlockSpec((length, 2 * dh), lambda b, h, i: (b, h)),
                  pl.BlockSpec((None, lc, 2 * dh), lambda b, h, i: (b, 0, h)),
                  pl.BlockSpec((None, lc, 2 * dh), lambda b, h, i: (b, 0, h)),
                  pl.BlockSpec((tq, dh), lambda b, h, i: (i, 0)),
                  pl.BlockSpec((tq, dh), lambda b, h, i: (i, 0))],
        out_specs=pl.BlockSpec((tq, 2 * dh), lambda b, h, i: (b * per_seq + i, h)),
        out_shape=jax.ShapeDtypeStruct((n_seq * length, d), BF16),
        compiler_params=_params(("parallel", "parallel", "arbitrary"), est),
        name="attn_latent",
    )(lam_p, subln.reshape(1, 2 * dh), qkv, kr, vb, ck, cv, cos, sin)


def _rope_tables(length, dh):
    half = dh // 2
    pos = jnp.arange(length)
    pos_r = (pos // GRID_COLS).astype(F32)
    pos_c = (pos % GRID_COLS).astype(F32)
    inv = jnp.power(ROPE_THETA, -jnp.arange(0, half, 2, dtype=F32) / half)
    ang_r = pos_r[:, None] * inv[None]
    ang_c = pos_c[:, None] * inv[None]
    cos = jnp.concatenate([jnp.cos(ang_r)] * 2 + [jnp.cos(ang_c)] * 2, axis=1)
    sin = jnp.concatenate([-jnp.sin(ang_r), jnp.sin(ang_r), -jnp.sin(ang_c), jnp.sin(ang_c)], axis=1)
    return cos, sin


def kernel(x_prompt, x_sample, cache_k, cache_v, c, c_ctx, mod_w, mod_b, ln1_g, ln1_b, ln2_g, ln2_b, pool_w, pool_scale, hyena_w_in, hyena_conv, hyena_ffn_w1, hyena_ffn_b1, hyena_ffn_w2, hyena_ffn_b2, hyena_ffn_w3, hyena_ffn_b3, hyena_freq, hyena_decay, hyena_skip, hyena_w_out, attn_w_qkv, attn_lambda, attn_subln, attn_w_out, ffn_w_up, ffn_conv, ffn_w_down):
    n_ctx, s_len, d = x_prompt.shape
    n_lat, ds, _ = x_sample.shape
    depth = mod_w.shape[0]
    tp, ts = n_ctx * s_len, n_lat * ds
    t = tp + ts
    dims = (tp, s_len, ds)
    alpha = (2 * depth) ** 0.25
    nh = cache_k.shape[3]
    dh = cache_k.shape[4] // 2
    lc = cache_k.shape[2]
    assert tp % ds == 0
    assert n_lat + 1 <= 8

    cond8 = jnp.zeros((8, d), F32).at[0].set(c_ctx).at[1:1 + n_lat].set(c)
    mods = _mods_call(cond8, mod_w, mod_b)
    mods4 = mods.reshape(depth, 8, 1, 6 * d)

    x = jnp.concatenate([x_prompt.reshape(tp, d), x_sample.reshape(ts, d)], axis=0)

    pos = jnp.concatenate([jnp.arange(tp) % s_len, jnp.arange(ts) % ds])
    last = jnp.concatenate([jnp.full((tp,), s_len - 1), jnp.full((ts,), ds - 1)])
    edge = jnp.stack([(pos != 0), (pos != last)], axis=1).astype(F32)

    w_down16 = ffn_w_down.astype(BF16)
    w_up16 = ffn_w_up.astype(BF16)
    hyena_in16 = hyena_w_in.astype(BF16)
    qkv16 = attn_w_qkv.astype(BF16)
    hyena_out16 = hyena_w_out.astype(BF16)
    attn_out16 = attn_w_out.astype(BF16)

    new_k = new_v = None
    h_in = None
    for i in range(depth):
        kind, j = i % N_MIXER_KINDS, i // N_MIXER_KINDS
        if kind == 0:
            x, h2 = _pool_call(x, mods4, i, pool_w[j].astype(BF16), pool_scale[j], ln1_g[i], ln1_b[i], dims, alpha)
        elif kind == 1:
            p_even, p_odd = _up_conv_call(h_in, edge, hyena_in16, hyena_conv, j)
            z2s = []
            for length, n_seq, row0 in ((s_len, n_ctx, 0), (ds, n_lat, tp // ds)):
                hk, asum = _filter_call(length, hyena_ffn_w1[j], hyena_ffn_b1[j], hyena_ffn_w2[j], hyena_ffn_b2[j],
                                        hyena_ffn_w3[j], hyena_ffn_b3[j], hyena_freq[j], hyena_decay[j])
                half = length // 2
                tq = _divisor(half, 256, SUBLANES_F32)
                wf, wft = _dft_tables(half, tq)
                phase = jnp.arange(half, dtype=F32) * (math.pi / length)
                tw = jnp.stack([jnp.cos(phase), jnp.sin(phase)], axis=1)
                planes = _kf_call(wf, tw, hk, asum, tq)
                tabs = (wf, wft, tw, tq)
                pe3 = p_even.reshape(t // length, half, 3 * d)
                po3 = p_odd.reshape(t // length, half, 3 * d)
                z1e, z1o = _dftconv_call(tabs, planes, 0, hyena_skip[j, 0:1], (pe3, row0, 0), (po3, row0, 0),
                                         (pe3, row0, 1), (po3, row0, 1), n_seq, d, False)
                z2, = _dftconv_call(tabs, planes, 1, hyena_skip[j, 1:2], (z1e, 0, 0), (z1o, 0, 0),
                                    (pe3, row0, 2), (po3, row0, 2), n_seq, d, True)
                z2s.append(z2.reshape(n_seq * length, d))
            z2 = jnp.concatenate(z2s, axis=0)
            x, h2 = _down_ln_call(z2, hyena_out16, j, x, mods4, i, 2, i, 3, ln1_g[i], ln1_b[i],
                                  dims, alpha, "hyena_out")
        else:
            lam_init = 0.8 - 0.6 * math.exp(-0.3 * i)
            qkv = _up_plain_call(h_in, qkv16, j)
            new_k = qkv[:tp, d:2 * d].reshape(n_ctx, s_len, nh, 2 * dh)
            new_v = qkv[:tp, 2 * d:3 * d].reshape(n_ctx, s_len, nh, 2 * dh)
            o_ctx = _attn_ctx_call(qkv, attn_lambda[j], attn_subln[j], n_ctx, s_len, d, dh, lam_init)
            cos, sin = _rope_tables(ds, dh)
            kr, vb = _rope_kv_call(qkv, cos, sin, tp, d, dh)
            o_lat = _attn_lat_call(qkv, kr, vb, cache_k[:, j].reshape(n_lat, lc, d), cache_v[:, j].reshape(n_lat, lc, d),
                                   cos, sin, attn_lambda[j], attn_subln[j], tp, n_lat, ds, d, dh, lam_init)
            o = jnp.concatenate([o_ctx, o_lat], axis=0)
            x, h2 = _down_ln_call(o, attn_out16, j, x, mods4, i, 2, i, 3, ln1_g[i], ln1_b[i],
                                  dims, alpha, "attn_out")
        a = _up_ffn_call(h2, edge, w_up16, ffn_conv, i)
        last_layer = i == depth - 1
        x, h_in = _down_ln_call(a, w_down16, i, x, mods4, i, 5, i + 1, None if last_layer else 0,
                                ln2_g[i], ln2_b[i], dims, alpha, "ffn_down")

    y_prompt = x[:tp].reshape(n_ctx, s_len, d)
    y_sample = x[tp:].reshape(n_lat, ds, d)
    return (y_prompt, y_sample, new_k[:, None], new_v[:, None])
```
